```python
import math
import jax
import jax.numpy as jnp
from jax import lax
import numpy as np

D_MODEL = 2048
BATCH = 8
SEQ = 4096
DEPTH = 4

ATT_PATTERNS = ((128, 1), (512, 4), (2048, 16))
N_GROUPS_A = len(ATT_PATTERNS)
HEADS_PER_GROUP = 8
HEAD_DIM = 64
N_HEADS_A = N_GROUPS_A * HEADS_PER_GROUP
QKV_WIDTH_A = N_HEADS_A * HEAD_DIM
WIDTH_A = HEADS_PER_GROUP * HEAD_DIM
ATT_BLOCK = 128
N_REL_BUCKETS = 32
REL_MAX_DIST = 2048
NEG_INF = -1e30
CHUNK = 128
WIDTH_B = 768
N_GROUPS_B = 6
GROUP_B = WIDTH_B // N_GROUPS_B
WIDTH_C = 768
SSM_GROUP = 16
N_GROUPS_C = WIDTH_C // SSM_GROUP
SSM_STATE = 64
DT_MIN = 1e-3
DT_MAX = 1e-1
N_BRANCH = 3
D_FF = -(-8 * D_MODEL // (3 * 256)) * 256
ALPHA = (2 * DEPTH) ** 0.25
BETA = (8 * DEPTH) ** -0.25
IN_SPLIT = (QKV_WIDTH_A, QKV_WIDTH_A, QKV_WIDTH_A, 2 * WIDTH_B, WIDTH_C, N_BRANCH * D_MODEL)
IN_COLS = sum(IN_SPLIT)

kernel_name = 'hybrid_dilated_attn_gmlp_s5_deepnorm'


def _layer_norm(x, g, b, eps=1e-5):
    xf = x.astype(jnp.float32)
    mu = jnp.mean(xf, axis=-1, keepdims=True)
    var = jnp.mean(jnp.square(xf - mu), axis=-1, keepdims=True)
    return ((xf - mu) * lax.rsqrt(var + eps) * g + b).astype(x.dtype)


def _t5_bucket(dist):
    max_exact = N_REL_BUCKETS // 2
    d = np.maximum(dist, 1).astype(np.float32)
    scale = (N_REL_BUCKETS - max_exact) / math.log(REL_MAX_DIST / max_exact)
    large = max_exact + (np.log(d / max_exact) * scale).astype(np.int32)
    large = np.minimum(large, N_REL_BUCKETS - 1)
    return np.where(dist < max_exact, dist, large).astype(np.int32)


def _band_steps():
    i = np.arange(ATT_BLOCK)[:, None]
    kk = np.arange(2 * ATT_BLOCK)[None, :]
    return ATT_BLOCK + i - kk


def _group_rel_bias(rel_bias, g, dilation):
    bucket = _t5_bucket(np.maximum(_band_steps(), 0) * dilation)
    cols = rel_bias[:, g * HEADS_PER_GROUP:(g + 1) * HEADS_PER_GROUP]
    return jnp.transpose(cols[bucket], (2, 0, 1)).astype(jnp.float32)


def _dilated_window_attention(q, k, v, bias, dilation, n_steps):
    bsz, s, h, hd = q.shape
    L = s // dilation
    nb = -(-L // ATT_BLOCK)
    lp = nb * ATT_BLOCK

    def to_streams(t):
        t = t.reshape(bsz, L, dilation, h, hd).transpose(0, 2, 1, 3, 4)
        return jnp.pad(t, ((0, 0), (0, 0), (0, lp - L), (0, 0), (0, 0)))

    def to_band(t):
        t = jnp.pad(t, ((0, 0), (0, 0), (ATT_BLOCK, 0), (0, 0), (0, 0)))
        t = t.reshape(bsz, dilation, nb + 1, ATT_BLOCK, h, hd)
        return jnp.concatenate([t[:, :, :-1], t[:, :, 1:]], axis=3)

    qb = to_streams(q).reshape(bsz, dilation, nb, ATT_BLOCK, h, hd)
    kb = to_band(to_streams(k))
    vb = to_band(to_streams(v))
    steps = _band_steps()
    key_idx = (np.arange(nb)[:, None] - 1) * ATT_BLOCK + np.arange(2 * ATT_BLOCK)[None, :]
    mask = ((steps >= 0) & (steps <= n_steps))[None, None] & (key_idx >= 0)[:, None, None, :]
    logits = jnp.einsum('brnqhd,brnkhd->brnhqk', qb, kb, preferred_element_type=jnp.float32)
    logits = jnp.where(mask, logits * (hd ** -0.5) + bias, NEG_INF)
    m = jnp.max(logits, axis=-1, keepdims=True)
    p = jnp.exp(logits - m)
    den = jnp.sum(p, axis=-1, keepdims=True)
    o = jnp.einsum('brnhqk,brnkhd->brnqhd', p, vb.astype(jnp.float32)) / jnp.swapaxes(den, 3, 4)
    lse = jnp.swapaxes((m + jnp.log(den))[..., 0], 3, 4)
    o = o.reshape(bsz, dilation, lp, h, hd)[:, :, :L].transpose(0, 2, 1, 3, 4).reshape(bsz, s, h, hd)
    lse = lse.reshape(bsz, dilation, lp, h)[:, :, :L].transpose(0, 2, 1, 3).reshape(bsz, s, h)
    return o, lse


def _spatial_gating(z, ln_g, ln_b, w_s, b_s):
    bsz, s, _ = z.shape
    u, v = jnp.split(z, 2, axis=-1)
    v = _layer_norm(v, ln_g, ln_b)
    vc = v.reshape(bsz, s // CHUNK, CHUNK, N_GROUPS_B, GROUP_B)
    w = jnp.tril(w_s)
    mixed = jnp.einsum('gts,bnsgc->bntgc', w, vc) + jnp.transpose(b_s)[:, :, None]
    return u * mixed.reshape(bsz, s, WIDTH_B)


def _ssm_combine(e1, e2):
    a1r, a1i, b1r, b1i = e1
    a2r, a2i, b2r, b2i = e2
    return (a2r * a1r - a2i * a1i,
            a2r * a1i + a2i * a1r,
            a2r * b1r - a2i * b1i + b2r,
            a2r * b1i + a2i * b1r + b2i)


def _s5(u, lam_re, lam_im, log_dt, b_re, b_im, c_re, c_im, d_skip):
    bsz, s, _ = u.shape
    f32 = jnp.float32
    uf = u.astype(f32)
    ug = uf.reshape(bsz, s, N_GROUPS_C, SSM_GROUP)
    lr = lam_re.astype(f32)
    li = lam_im.astype(f32)
    dt = jnp.exp(log_dt.astype(f32))[:, None]
    mag = jnp.exp(lr * dt)
    ab_re = mag * jnp.cos(li * dt)
    ab_im = mag * jnp.sin(li * dt)
    nrm = lr * lr + li * li
    cr = ((ab_re - 1.0) * lr + ab_im * li) / nrm
    ci = (ab_im * lr - (ab_re - 1.0) * li) / nrm
    bb_re = cr[..., None] * b_re - ci[..., None] * b_im
    bb_im = cr[..., None] * b_im + ci[..., None] * b_re
    bu_re = jnp.einsum('bsgh,gph->bsgp', ug, bb_re.astype(f32))
    bu_im = jnp.einsum('bsgh,gph->bsgp', ug, bb_im.astype(f32))
    a_re = jnp.broadcast_to(ab_re[None, None], (1, s, N_GROUPS_C, SSM_STATE))
    a_im = jnp.broadcast_to(ab_im[None, None], (1, s, N_GROUPS_C, SSM_STATE))
    _, _, xr, xi = lax.associative_scan(_ssm_combine, (a_re, a_im, bu_re, bu_im), axis=1)
    y = (jnp.einsum('bsgp,ghp->bsgh', xr, c_re.astype(f32))
         - jnp.einsum('bsgp,ghp->bsgh', xi, c_im.astype(f32)))
    return y.reshape(bsz, s, WIDTH_C) + d_skip.astype(f32) * uf


def setup_inputs(seed: int = 0) -> dict:
    key = jax.random.key(seed)
    ks = iter(jax.random.split(key, 32))
    nrm = lambda shape, scale: jax.random.normal(next(ks), shape, jnp.float32) * scale
    gain = lambda shape: 1.0 + nrm(shape, 0.02)
    n_idx = jnp.arange(SSM_STATE, dtype=jnp.float32)
    return {
        'x': nrm((BATCH, SEQ, D_MODEL), 1.0),
        'w_in': nrm((DEPTH, D_MODEL, IN_COLS), D_MODEL ** -0.5),
        'b_in': nrm((DEPTH, IN_COLS), 0.02),
        'rel_bias': nrm((N_REL_BUCKETS, N_HEADS_A), 0.1),
        'sgu_ln_g': gain((DEPTH, WIDTH_B)),
        'sgu_ln_b': nrm((DEPTH, WIDTH_B), 0.02),
        'w_s': nrm((DEPTH, N_GROUPS_B, CHUNK, CHUNK), CHUNK ** -0.5),
        'b_s': gain((DEPTH, N_GROUPS_B, CHUNK)),
        'lam_re': -0.5 + nrm((DEPTH, N_GROUPS_C, SSM_STATE), 0.01),
        'lam_im': math.pi * n_idx + nrm((DEPTH, N_GROUPS_C, SSM_STATE), 0.01),
        'log_dt': jax.random.uniform(next(ks), (DEPTH, N_GROUPS_C), jnp.float32,
                                     minval=math.log(DT_MIN), maxval=math.log(DT_MAX)),
        'b_re': nrm((DEPTH, N_GROUPS_C, SSM_STATE, SSM_GROUP), (2 * SSM_GROUP) ** -0.5),
        'b_im': nrm((DEPTH, N_GROUPS_C, SSM_STATE, SSM_GROUP), (2 * SSM_GROUP) ** -0.5),
        'c_re': nrm((DEPTH, N_GROUPS_C, SSM_GROUP, SSM_STATE), SSM_STATE ** -0.5),
        'c_im': nrm((DEPTH, N_GROUPS_C, SSM_GROUP, SSM_STATE), SSM_STATE ** -0.5),
        'd_skip': nrm((DEPTH, WIDTH_C), 1.0),
        'w_glu': nrm((DEPTH, WIDTH_C, WIDTH_C), WIDTH_C ** -0.5),
        'b_glu': nrm((DEPTH, WIDTH_C), 0.02),
        'w_pa': nrm((DEPTH, WIDTH_A, D_MODEL), WIDTH_A ** -0.5),
        'w_pb': nrm((DEPTH, WIDTH_B, D_MODEL), WIDTH_B ** -0.5),
        'w_pc': nrm((DEPTH, WIDTH_C, D_MODEL), WIDTH_C ** -0.5),
        'w_o': nrm((DEPTH, D_MODEL, D_MODEL), BETA * D_MODEL ** -0.5),
        'ln1_g': gain((DEPTH, D_MODEL)),
        'ln1_b': nrm((DEPTH, D_MODEL), 0.02),
        'w_ffn_in': nrm((DEPTH, D_MODEL, 2 * D_FF), D_MODEL ** -0.5),
        'w_ffn_out': nrm((DEPTH, D_FF, D_MODEL), BETA * D_FF ** -0.5),
        'ln2_g': gain((DEPTH, D_MODEL)),
        'ln2_b': nrm((DEPTH, D_MODEL), 0.02),
    }


def reference(x, w_in, b_in, rel_bias, sgu_ln_g, sgu_ln_b, w_s, b_s, lam_re, lam_im, log_dt,
              b_re, b_im, c_re, c_im, d_skip, w_glu, b_glu, w_pa, w_pb, w_pc, w_o,
              ln1_g, ln1_b, w_ffn_in, w_ffn_out, ln2_g, ln2_b):
    dt = x.dtype
    bsz, s, _ = x.shape
    offs = np.cumsum(IN_SPLIT)[:-1].tolist()
    group_bias = [_group_rel_bias(rel_bias, g, dil) for g, (_, dil) in enumerate(ATT_PATTERNS)]
    for l in range(DEPTH):
        proj = x @ w_in[l] + b_in[l]
        q, k, v, zb, uc, gl = jnp.split(proj, offs, axis=-1)
        q = q.reshape(bsz, s, N_HEADS_A, HEAD_DIM)
        k = k.reshape(bsz, s, N_HEADS_A, HEAD_DIM)
        v = v.reshape(bsz, s, N_HEADS_A, HEAD_DIM)
        outs, lses = [], []
        for g, (window, dil) in enumerate(ATT_PATTERNS):
            sl = slice(g * HEADS_PER_GROUP, (g + 1) * HEADS_PER_GROUP)
            o_g, lse_g = _dilated_window_attention(q[:, :, sl], k[:, :, sl], v[:, :, sl],
                                                   group_bias[g], dil, window // dil)
            outs.append(o_g)
            lses.append(lse_g)
        wts = jax.nn.softmax(jnp.stack(lses, axis=0), axis=0)
        ya = jnp.sum(wts[..., None] * jnp.stack(outs, axis=0), axis=0)
        ya = ya.reshape(bsz, s, WIDTH_A).astype(dt)
        yb = _spatial_gating(jax.nn.gelu(zb), sgu_ln_g[l], sgu_ln_b[l], w_s[l], b_s[l]).astype(dt)
        yc = jax.nn.gelu(_s5(uc, lam_re[l], lam_im[l], log_dt[l], b_re[l], b_im[l],
                             c_re[l], c_im[l], d_skip[l]))
        yc = (yc * jax.nn.sigmoid(yc @ w_glu[l] + b_glu[l])).astype(dt)
        gates = jax.nn.sigmoid(gl.reshape(bsz, s, N_BRANCH, D_MODEL))
        merged = (gates[:, :, 0] * (ya @ w_pa[l]) + gates[:, :, 1] * (yb @ w_pb[l])
                  + gates[:, :, 2] * (yc @ w_pc[l]))
        x = _layer_norm(ALPHA * x + merged @ w_o[l], ln1_g[l], ln1_b[l]).astype(dt)
        gate_f, up = jnp.split(x @ w_ffn_in[l], 2, axis=-1)
        f = (jax.nn.silu(gate_f) * up) @ w_ffn_out[l]
        x = _layer_norm(ALPHA * x + f, ln2_g[l], ln2_b[l]).astype(dt)
    return x
```

```python
import functools
import math

import numpy as np
import jax
import jax.numpy as jnp
from jax import lax
from jax.experimental import pallas as pl
from jax.experimental.pallas import tpu as pltpu

F32 = jnp.float32
BF16 = jnp.bfloat16

ATT_PATTERNS = ((128, 1), (512, 4), (2048, 16))
HEADS_PER_GROUP = 8
HEAD_DIM = 64
N_HEADS = len(ATT_PATTERNS) * HEADS_PER_GROUP
QKV_WIDTH = N_HEADS * HEAD_DIM
WIDTH_A = HEADS_PER_GROUP * HEAD_DIM
ATT_BLOCK = 128
N_REL_BUCKETS = 32
REL_MAX_DIST = 2048
NEG_INF = -1e30
CHUNK = 128
WIDTH_B = 768
N_GROUPS_B = 6
WIDTH_C = 768
SSM_GROUP = 16
N_GROUPS_C = WIDTH_C // SSM_GROUP
SSM_STATE = 64
N_BRANCH = 3
LN_EPS = 1e-5

LANES = 128
SUBLANES = 8
VMEM_LIMIT = 48 * 1024 * 1024

SLAB = LANES
GROUPS_PER_SLAB = SLAB // SSM_GROUP
N_SLABS = WIDTH_C // SLAB
SLAB_STATES = GROUPS_PER_SLAB * SSM_STATE


def _params(sem, vmem=VMEM_LIMIT):
    return pltpu.CompilerParams(dimension_semantics=sem, vmem_limit_bytes=vmem)


def _layer_norm_rows(y, g, b):
    mu = jnp.mean(y, axis=-1, keepdims=True)
    yc = y - mu
    var = jnp.mean(yc * yc, axis=-1, keepdims=True)
    return yc * lax.rsqrt(var + LN_EPS) * g + b


def _mm_bias_kernel(x_ref, w_ref, b_ref, o_ref):
    acc = jnp.dot(x_ref[...], w_ref[...], preferred_element_type=F32)
    o_ref[...] = (acc + b_ref[...]).astype(o_ref.dtype)


def _matmul_bias(x, w, b, layer, col_map, n_out, out_dtype, bm, bn):
    m, k = x.shape
    return pl.pallas_call(
        _mm_bias_kernel,
        grid=(n_out // bn, m // bm),
        in_specs=[
            pl.BlockSpec((bm, k), lambda j, i: (i, 0)),
            pl.BlockSpec((None, k, bn), lambda j, i: (layer, 0, col_map(j))),
            pl.BlockSpec((None, 1, bn), lambda j, i: (layer, 0, col_map(j))),
        ],
        out_specs=pl.BlockSpec((bm, bn), lambda j, i: (i, j)),
        out_shape=jax.ShapeDtypeStruct((m, n_out), out_dtype),
        compiler_params=_params(("arbitrary", "arbitrary")),
        name="in_proj",
    )(x, w, b)


def _attn_kernel(q_ref, k_ref, v_ref, kp_ref, vp_ref, bias_ref, o_ref, lse_ref, kf_ref, vf_ref,
                 *, nblk, n_steps):
    q_blk = ATT_BLOCK
    i = pl.program_id(2)
    kf_ref[0:q_blk, :] = kp_ref[...]
    kf_ref[q_blk:, :] = k_ref[...]
    vf_ref[0:q_blk, :] = vp_ref[...]
    vf_ref[q_blk:, :] = v_ref[...]
    row = lax.broadcasted_iota(jnp.int32, (q_blk, 2 * q_blk), 0)
    col = lax.broadcasted_iota(jnp.int32, (q_blk, 2 * q_blk), 1)
    steps = q_blk + row - col
    band = jnp.logical_and(steps >= 0, steps <= n_steps)
    lane = lax.broadcasted_iota(jnp.int32, (q_blk, LANES), 1)
    lo = lane < HEAD_DIM
    scale = HEAD_DIM ** -0.5

    def body(n, carry):
        r0 = pl.multiple_of(n * q_blk, q_blk)
        first_col = jnp.where(jnp.logical_and(i == 0, n == 0), q_blk, 0)
        mask = jnp.logical_and(band, col >= first_col)
        lse_tile = jnp.zeros((q_blk, LANES), F32)
        for pr in range(HEADS_PER_GROUP // 2):
            cs = slice(pr * LANES, (pr + 1) * LANES)
            q2 = q_ref[pl.ds(r0, q_blk), cs]
            k2 = kf_ref[pl.ds(r0, 2 * q_blk), cs]
            v2 = vf_ref[pl.ds(r0, 2 * q_blk), cs]
            outs = []
            for hh in range(2):
                h = 2 * pr + hh
                sel = lo if hh == 0 else jnp.logical_not(lo)
                qm = jnp.where(sel, q2, jnp.zeros_like(q2))
                s = lax.dot_general(qm, k2, (((1,), (1,)), ((), ())), preferred_element_type=F32)
                s = jnp.where(mask, s * scale + bias_ref[h], NEG_INF)
                m = jnp.max(s, axis=-1, keepdims=True)
                p = jnp.exp(s - m)
                den = jnp.sum(p, axis=-1, keepdims=True)
                o = jnp.dot(p.astype(BF16), v2, preferred_element_type=F32) / den
                outs.append(o)
                lse_tile = jnp.where(lane == h, m + jnp.log(den), lse_tile)
            o_ref[pl.ds(r0, q_blk), cs] = jnp.where(lo, outs[0], outs[1]).astype(o_ref.dtype)
        lse_ref[pl.ds(r0, q_blk), :] = lse_tile
        return carry

    lax.fori_loop(0, nblk, body, 0)


def _attention_group(qkv, bias, g, dilation, n_steps, bsz, seq):
    r = dilation
    length = seq // r
    assert length % ATT_BLOCK == 0
    lb = min(4 * ATT_BLOCK, length)
    nblk = lb // ATT_BLOCK
    nsteps = length // lb
    cpb = 3 * QKV_WIDTH // WIDTH_A
    qkv_v = qkv.reshape(bsz, length, r * 3 * QKV_WIDTH)

    def col(which):
        return lambda b, c, i: (b, i, c * cpb + which * (QKV_WIDTH // WIDTH_A) + g)

    def col_prev(which):
        return lambda b, c, i: (b, jnp.maximum(i * nblk - 1, 0), c * cpb + which * (QKV_WIDTH // WIDTH_A) + g)

    o, lse = pl.pallas_call(
        functools.partial(_attn_kernel, nblk=nblk, n_steps=n_steps),
        grid=(bsz, r, nsteps),
        in_specs=[
            pl.BlockSpec((None, lb, WIDTH_A), col(0)),
            pl.BlockSpec((None, lb, WIDTH_A), col(1)),
            pl.BlockSpec((None, lb, WIDTH_A), col(2)),
            pl.BlockSpec((None, ATT_BLOCK, WIDTH_A), col_prev(1)),
            pl.BlockSpec((None, ATT_BLOCK, WIDTH_A), col_prev(2)),
            pl.BlockSpec((HEADS_PER_GROUP, ATT_BLOCK, 2 * ATT_BLOCK), lambda b, c, i: (0, 0, 0)),
        ],
        out_specs=[
            pl.BlockSpec((None, lb, WIDTH_A), lambda b, c, i: (b, i, c)),
            pl.BlockSpec((None, lb, LANES), lambda b, c, i: (b, i, c)),
        ],
        out_shape=[
            jax.ShapeDtypeStruct((bsz, length, r * WIDTH_A), F32),
            jax.ShapeDtypeStruct((bsz, length, r * LANES), F32),
        ],
        scratch_shapes=[
            pltpu.VMEM((lb + ATT_BLOCK, WIDTH_A), BF16),
            pltpu.VMEM((lb + ATT_BLOCK, WIDTH_A), BF16),
        ],
        compiler_params=_params(("arbitrary", "arbitrary", "arbitrary")),
        name=f"attn_d{r}",
    )(qkv_v, qkv_v, qkv_v, qkv_v, qkv_v, bias)
    return o.reshape(bsz * seq, WIDTH_A), lse.reshape(bsz * seq, LANES)


def _t5_bucket(dist):
    max_exact = N_REL_BUCKETS // 2
    d = np.maximum(dist, 1).astype(np.float32)
    scale = (N_REL_BUCKETS - max_exact) / math.log(REL_MAX_DIST / max_exact)
    large = max_exact + (np.log(d / max_exact) * scale).astype(np.int32)
    large = np.minimum(large, N_REL_BUCKETS - 1)
    return np.where(dist < max_exact, dist, large).astype(np.int32)


def _group_rel_bias(rel_bias, g, dilation):
    i = np.arange(ATT_BLOCK)[:, None]
    kk = np.arange(2 * ATT_BLOCK)[None, :]
    steps = ATT_BLOCK + i - kk
    bucket = _t5_bucket(np.maximum(steps, 0) * dilation)
    cols = rel_bias[:, g * HEADS_PER_GROUP:(g + 1) * HEADS_PER_GROUP]
    return jnp.transpose(cols[bucket], (2, 0, 1)).astype(F32)


def _gmlp_kernel(z_ref, g_ref, b_ref, w_ref, bs_ref, o_ref, *, nchunk):
    z = jax.nn.gelu(z_ref[...])
    u = z[:, :WIDTH_B]
    v = _layer_norm_rows(z[:, WIDTH_B:], g_ref[...], b_ref[...]).astype(BF16)
    row = lax.broadcasted_iota(jnp.int32, (CHUNK, CHUNK), 0)
    col = lax.broadcasted_iota(jnp.int32, (CHUNK, CHUNK), 1)
    causal = row >= col
    bs = bs_ref[...]
    for g in range(N_GROUPS_B):
        cs = slice(g * LANES, (g + 1) * LANES)
        w = jnp.where(causal, w_ref[g], 0.0).astype(BF16)
        rhs = jnp.concatenate([v[n * CHUNK:(n + 1) * CHUNK, cs] for n in range(nchunk)], axis=1)
        mixed = jnp.dot(w, rhs, preferred_element_type=F32)
        for n in range(nchunk):
            rs = slice(n * CHUNK, (n + 1) * CHUNK)
            o_ref[rs, cs] = (u[rs, cs] * (mixed[:, n * LANES:(n + 1) * LANES] + bs[:, cs])).astype(o_ref.dtype)


def _spatial_gating(rest, zb_col, ln_g, ln_b, w_s, b_s, layer, tm):
    t = rest.shape[0]
    bs = jnp.repeat(jnp.transpose(b_s[layer]), LANES, axis=1)
    return pl.pallas_call(
        functools.partial(_gmlp_kernel, nchunk=tm // CHUNK),
        grid=(t // tm,),
        in_specs=[
            pl.BlockSpec((tm, 2 * WIDTH_B), lambda i: (i, zb_col // (2 * WIDTH_B))),
            pl.BlockSpec((None, 1, WIDTH_B), lambda i: (layer, 0, 0)),
            pl.BlockSpec((None, 1, WIDTH_B), lambda i: (layer, 0, 0)),
            pl.BlockSpec((None, N_GROUPS_B, CHUNK, CHUNK), lambda i: (layer, 0, 0, 0)),
            pl.BlockSpec((CHUNK, WIDTH_B), lambda i: (0, 0)),
        ],
        out_specs=pl.BlockSpec((tm, WIDTH_B), lambda i: (i, 0)),
        out_shape=jax.ShapeDtypeStruct((t, WIDTH_B), BF16),
        compiler_params=_params(("arbitrary",)),
        name="gmlp",
    )(rest, ln_g, ln_b, w_s, bs)


N_POW = LANES


def _s5_prep_kernel(lr_ref, li_ref, dt_ref, bre_ref, bim_ref, pre_ref, pim_ref, bbre_ref, bbim_ref):
    lr = lr_ref[...]
    li = li_ref[...]
    dt = jnp.exp(dt_ref[...])
    kpow = (lax.broadcasted_iota(jnp.int32, (1, N_POW), 1) + 1).astype(F32)
    mag = jnp.exp(lr * dt * kpow)
    ang = li * dt * kpow
    pre_ref[...] = mag * jnp.cos(ang)
    pim_ref[...] = mag * jnp.sin(ang)
    mag1 = jnp.exp(lr * dt)
    ab_re = mag1 * jnp.cos(li * dt)
    ab_im = mag1 * jnp.sin(li * dt)
    nrm = lr * lr + li * li
    cr = ((ab_re - 1.0) * lr + ab_im * li) / nrm
    ci = (ab_im * lr - (ab_re - 1.0) * li) / nrm
    bre = bre_ref[...]
    bim = bim_ref[...]
    bbre_ref[...] = cr * bre - ci * bim
    bbim_ref[...] = cr * bim + ci * bre


def _s5_prep(lam_re, lam_im, log_dt, b_re, b_im):
    gp = N_GROUPS_C * SSM_STATE
    col = lambda a: a.reshape(gp, 1).astype(F32)
    dt = jnp.repeat(log_dt.astype(F32), SSM_STATE).reshape(gp, 1)
    vm = lambda shp: pl.BlockSpec(shp, lambda: (0,) * len(shp))
    return pl.pallas_call(
        _s5_prep_kernel,
        in_specs=[vm((gp, 1)), vm((gp, 1)), vm((gp, 1)), vm((gp, SSM_GROUP)), vm((gp, SSM_GROUP))],
        out_specs=[vm((gp, N_POW)), vm((gp, N_POW)), vm((gp, SSM_GROUP)), vm((gp, SSM_GROUP))],
        out_shape=[jax.ShapeDtypeStruct((gp, N_POW), F32), jax.ShapeDtypeStruct((gp, N_POW), F32),
                   jax.ShapeDtypeStruct((gp, SSM_GROUP), F32), jax.ShapeDtypeStruct((gp, SSM_GROUP), F32)],
        name="s5_prep",
    )(col(lam_re), col(lam_im), dt, b_re.reshape(gp, SSM_GROUP).astype(F32), b_im.reshape(gp, SSM_GROUP).astype(F32))


def _s5_kernel(u_ref, bmat_ref, cmat_ref, coef_ref, d_ref, o_ref, carry_ref, *, rows):
    j = pl.program_id(2)

    @pl.when(j == 0)
    def _():
        carry_ref[...] = jnp.zeros_like(carry_ref)

    ns = SLAB_STATES
    nv = rows // SUBLANES
    u = u_ref[...]
    bu = jnp.dot(u.astype(BF16), bmat_ref[...], preferred_element_type=F32)
    xr = bu[:, :ns].reshape(nv, SUBLANES, ns)
    xi = bu[:, ns:].reshape(nv, SUBLANES, ns)
    for idx, k in enumerate((1, 2, 4)):
        cr = coef_ref[2 * idx]
        ci = coef_ref[2 * idx + 1]
        sr = pltpu.roll(xr, k, 1)
        si = pltpu.roll(xi, k, 1)
        xr, xi = xr + (cr * sr - ci * si), xi + (cr * si + ci * sr)
    pr = coef_ref[6]
    pi = coef_ref[7]
    c_r = carry_ref[0:1, :]
    c_i = carry_ref[1:2, :]
    out_r, out_i = [], []
    for r in range(nv):
        yr = xr[r] + (pr * c_r - pi * c_i)
        yi = xi[r] + (pr * c_i + pi * c_r)
        out_r.append(yr)
        out_i.append(yi)
        c_r = yr[SUBLANES - 1:SUBLANES, :]
        c_i = yi[SUBLANES - 1:SUBLANES, :]
    carry_ref[0:1, :] = c_r
    carry_ref[1:2, :] = c_i
    x = jnp.concatenate([jnp.concatenate(out_r, axis=0), jnp.concatenate(out_i, axis=0)], axis=1)
    y = jnp.dot(x.astype(BF16), cmat_ref[...], preferred_element_type=F32)
    o_ref[...] = jax.nn.gelu(y + d_ref[...] * u).astype(o_ref.dtype)


def _s5_matrices(pw_re, pw_im, bb_re, bb_im, c_re, c_im):
    eye = jnp.eye(GROUPS_PER_SLAB, dtype=F32)
    g8, p, h = GROUPS_PER_SLAB, SSM_STATE, SSM_GROUP
    bbr = bb_re.reshape(N_SLABS, g8, p, h)
    bbi = bb_im.reshape(N_SLABS, g8, p, h)

    def bdiag_in(bb):
        return jnp.einsum("sgph,gk->sghkp", bb, eye).reshape(N_SLABS, g8 * h, g8 * p)

    bmat = jnp.concatenate([bdiag_in(bbr), bdiag_in(bbi)], axis=2).astype(BF16)
    cr = c_re.astype(F32).reshape(N_SLABS, g8, h, p)
    ci = c_im.astype(F32).reshape(N_SLABS, g8, h, p)

    def bdiag_out(c):
        return jnp.einsum("sghp,gk->sgpkh", c, eye).reshape(N_SLABS, g8 * p, g8 * h)

    cmat = jnp.concatenate([bdiag_out(cr), -bdiag_out(ci)], axis=1).astype(BF16)
    pr = pw_re.reshape(N_SLABS, SLAB_STATES, N_POW)
    pi = pw_im.reshape(N_SLABS, SLAB_STATES, N_POW)
    rowi = jnp.arange(SUBLANES)[None, :, None]
    coefs = []
    for k in (1, 2, 4):
        for pw in (pr, pi):
            a = jnp.broadcast_to(pw[:, :, k - 1][:, None, :], (N_SLABS, SUBLANES, SLAB_STATES))
            coefs.append(jnp.where(rowi >= k, a, 0.0))
    for pw in (pr, pi):
        coefs.append(jnp.transpose(pw[:, :, :SUBLANES], (0, 2, 1)))
    return bmat, cmat, jnp.stack(coefs, axis=1)


def _s5(rest, uc_col, bmat, cmat, coef, d_skip, layer, bsz, seq, rows):
    rest3 = rest.reshape(bsz, seq, rest.shape[-1])
    c0 = uc_col // SLAB
    out = pl.pallas_call(
        functools.partial(_s5_kernel, rows=rows),
        grid=(bsz, N_SLABS, seq // rows),
        in_specs=[
            pl.BlockSpec((None, rows, SLAB), lambda b, s, j: (b, j, c0 + s)),
            pl.BlockSpec((None, SLAB, 2 * SLAB_STATES), lambda b, s, j: (s, 0, 0)),
            pl.BlockSpec((None, 2 * SLAB_STATES, SLAB), lambda b, s, j: (s, 0, 0)),
            pl.BlockSpec((None, 8, SUBLANES, SLAB_STATES), lambda b, s, j: (s, 0, 0, 0)),
            pl.BlockSpec((None, 1, SLAB), lambda b, s, j: (layer, 0, s)),
        ],
        out_specs=pl.BlockSpec((None, rows, SLAB), lambda b, s, j: (b, j, s)),
        out_shape=jax.ShapeDtypeStruct((bsz, seq, WIDTH_C), F32),
        scratch_shapes=[pltpu.VMEM((SUBLANES, SLAB_STATES), F32)],
        compiler_params=_params(("arbitrary", "arbitrary", "arbitrary")),
        name="s5_scan",
    )(rest3, bmat, cmat, coef, d_skip)
    return out.reshape(bsz * seq, WIDTH_C)


def _merge_kernel(o0_ref, o1_ref, o2_ref, l0_ref, l1_ref, l2_ref, yb_ref, yc_ref,
                  g0_ref, g1_ref, g2_ref, e_ref, wpa_ref, wpb_ref, wpc_ref, wglu_ref, bglu_ref, out_ref):
    l0, l1, l2 = l0_ref[...], l1_ref[...], l2_ref[...]
    m = jnp.maximum(jnp.maximum(l0, l1), l2)
    e0, e1, e2 = jnp.exp(l0 - m), jnp.exp(l1 - m), jnp.exp(l2 - m)
    den = e0 + e1 + e2
    expand = lambda w: jnp.dot(w, e_ref[...], preferred_element_type=F32, precision=lax.Precision.HIGHEST)
    ya = (expand(e0 / den) * o0_ref[...] + expand(e1 / den) * o1_ref[...] + expand(e2 / den) * o2_ref[...])
    pa = jnp.dot(ya.astype(BF16), wpa_ref[...], preferred_element_type=F32)
    pb = jnp.dot(yb_ref[...], wpb_ref[...], preferred_element_type=F32)
    yc = yc_ref[...]
    glu = yc * jax.nn.sigmoid(jnp.dot(yc.astype(BF16), wglu_ref[...], preferred_element_type=F32) + bglu_ref[...])
    pc = jnp.dot(glu.astype(BF16), wpc_ref[...], preferred_element_type=F32)
    merged = (jax.nn.sigmoid(g0_ref[...]) * pa + jax.nn.sigmoid(g1_ref[...]) * pb
              + jax.nn.sigmoid(g2_ref[...]) * pc)
    out_ref[...] = merged.astype(out_ref.dtype)


def _merge(outs, lses, yb, yc, rest, gl_col, expand, w_pa, w_pb, w_pc, w_glu, b_glu, layer, d_model, tm):
    t = yb.shape[0]
    gcol = gl_col // d_model
    row = lambda w: pl.BlockSpec((tm, w), lambda i: (i, 0))
    full = lambda a, b: pl.BlockSpec((None, a, b), lambda i: (layer, 0, 0))
    return pl.pallas_call(
        _merge_kernel,
        grid=(t // tm,),
        in_specs=[row(WIDTH_A)] * 3 + [row(LANES)] * 3 + [row(WIDTH_B), row(WIDTH_C)]
        + [pl.BlockSpec((tm, d_model), lambda i, k=k: (i, gcol + k)) for k in range(N_BRANCH)]
        + [pl.BlockSpec((LANES, WIDTH_A), lambda i: (0, 0)),
           full(WIDTH_A, d_model), full(WIDTH_B, d_model), full(WIDTH_C, d_model),
           full(WIDTH_C, WIDTH_C), full(1, WIDTH_C)],
        out_specs=pl.BlockSpec((tm, d_model), lambda i: (i, 0)),
        out_shape=jax.ShapeDtypeStruct((t, d_model), BF16),
        compiler_params=_params(("arbitrary",)),
        name="merge",
    )(*outs, *lses, yb, yc, rest, rest, rest, expand, w_pa, w_pb, w_pc, w_glu, b_glu)


def _proj_ln_kernel(m_ref, x_ref, w_ref, g_ref, b_ref, o_ref, ob_ref, *, alpha):
    f = jnp.dot(m_ref[...], w_ref[...], preferred_element_type=F32)
    y = _layer_norm_rows(alpha * x_ref[...] + f, g_ref[...], b_ref[...])
    o_ref[...] = y
    ob_ref[...] = y.astype(BF16)


def _proj_ln(merged, x, w_o, ln_g, ln_b, layer, alpha, tm):
    t, d = x.shape
    return pl.pallas_call(
        functools.partial(_proj_ln_kernel, alpha=alpha),
        grid=(t // tm,),
        in_specs=[
            pl.BlockSpec((tm, d), lambda i: (i, 0)),
            pl.BlockSpec((tm, d), lambda i: (i, 0)),
            pl.BlockSpec((None, d, d), lambda i: (layer, 0, 0)),
            pl.BlockSpec((None, 1, d), lambda i: (layer, 0, 0)),
            pl.BlockSpec((None, 1, d), lambda i: (layer, 0, 0)),
        ],
        out_specs=[pl.BlockSpec((tm, d), lambda i: (i, 0)), pl.BlockSpec((tm, d), lambda i: (i, 0))],
        out_shape=[jax.ShapeDtypeStruct((t, d), F32), jax.ShapeDtypeStruct((t, d), BF16)],
        compiler_params=_params(("arbitrary",)),
        name="out_proj_ln",
    )(merged, x, w_o, ln_g, ln_b)


def _swiglu_kernel(x_ref, wg_ref, wu_ref, o_ref):
    x = x_ref[...]
    gate = jnp.dot(x, wg_ref[...], preferred_element_type=F32)
    up = jnp.dot(x, wu_ref[...], preferred_element_type=F32)
    o_ref[...] = (jax.nn.silu(gate) * up).astype(o_ref.dtype)


def _swiglu(xb, w_in, layer, d_ff, bm, bn):
    t, d = xb.shape
    nj = d_ff // bn
    return pl.pallas_call(
        _swiglu_kernel,
        grid=(nj, t // bm),
        in_specs=[
            pl.BlockSpec((bm, d), lambda j, i: (i, 0)),
            pl.BlockSpec((None, d, bn), lambda j, i: (layer, 0, j)),
            pl.BlockSpec((None, d, bn), lambda j, i: (layer, 0, j + nj)),
        ],
        out_specs=pl.BlockSpec((bm, bn), lambda j, i: (i, j)),
        out_shape=jax.ShapeDtypeStruct((t, d_ff), BF16),
        compiler_params=_params(("arbitrary", "arbitrary")),
        name="ffn_in",
    )(xb, w_in, w_in)


def _ffn_out_ln_kernel(h_ref, w_ref, x_ref, g_ref, b_ref, o_ref, ob_ref, acc_ref, *, alpha, nk):
    k = pl.program_id(1)
    part = jnp.dot(h_ref[...], w_ref[...], preferred_element_type=F32)

    @pl.when(k == 0)
    def _():
        acc_ref[...] = part

    @pl.when(k > 0)
    def _():
        acc_ref[...] += part

    @pl.when(k == nk - 1)
    def _():
        y = _layer_norm_rows(alpha * x_ref[...] + acc_ref[...], g_ref[...], b_ref[...])
        o_ref[...] = y
        ob_ref[...] = y.astype(BF16)


def _ffn_out_ln(h, x, w_out, ln_g, ln_b, layer, alpha, bm, bk):
    t, d = x.shape
    d_ff = h.shape[1]
    nk = d_ff // bk
    return pl.pallas_call(
        functools.partial(_ffn_out_ln_kernel, alpha=alpha, nk=nk),
        grid=(t // bm, nk),
        in_specs=[
            pl.BlockSpec((bm, bk), lambda i, k: (i, k)),
            pl.BlockSpec((None, bk, d), lambda i, k: (layer, k, 0)),
            pl.BlockSpec((bm, d), lambda i, k: (i, 0)),
            pl.BlockSpec((None, 1, d), lambda i, k: (layer, 0, 0)),
            pl.BlockSpec((None, 1, d), lambda i, k: (layer, 0, 0)),
        ],
        out_specs=[pl.BlockSpec((bm, d), lambda i, k: (i, 0)), pl.BlockSpec((bm, d), lambda i, k: (i, 0))],
        out_shape=[jax.ShapeDtypeStruct((t, d), F32), jax.ShapeDtypeStruct((t, d), BF16)],
        scratch_shapes=[pltpu.VMEM((bm, d), F32)],
        compiler_params=_params(("arbitrary", "arbitrary")),
        name="ffn_out_ln",
    )(h, w_out, x, ln_g, ln_b)


@jax.jit
def kernel(x, w_in, b_in, rel_bias, sgu_ln_g, sgu_ln_b, w_s, b_s, lam_re, lam_im, log_dt, b_re, b_im, c_re, c_im, d_skip, w_glu, b_glu, w_pa, w_pb, w_pc, w_o, ln1_g, ln1_b, w_ffn_in, w_ffn_out, ln2_g, ln2_b):
    bsz, seq, d_model = x.shape
    depth = w_in.shape[0]
    d_ff = w_ffn_out.shape[1]
    t = bsz * seq
    alpha = (2 * depth) ** 0.25
    in_cols = w_in.shape[-1]
    gl_width = N_BRANCH * d_model
    zb_src = 3 * QKV_WIDTH
    uc_src = zb_src + 2 * WIDTH_B
    gl_src = uc_src + WIDTH_C
    assert gl_src + gl_width == in_cols
    bn = 768
    rest_width = gl_width + 2 * WIDTH_B + WIDTH_C
    gl_blocks = gl_width // bn
    zb_col, uc_col, gl_col = gl_width, gl_width + 2 * WIDTH_B, 0

    def rest_cols(j):
        return jnp.where(j < gl_blocks, j + gl_src // bn, j - gl_blocks + zb_src // bn)

    bf = lambda a: a.astype(BF16)
    row3 = lambda a: a.reshape(a.shape[0], 1, a.shape[1]).astype(F32)
    w_in_b, w_pa_b, w_pb_b, w_pc_b, w_glu_b = bf(w_in), bf(w_pa), bf(w_pb), bf(w_pc), bf(w_glu)
    w_o_b, w_ffn_in_b, w_ffn_out_b = bf(w_o), bf(w_ffn_in), bf(w_ffn_out)
    b_in3, b_glu3, d_skip3 = row3(b_in), row3(b_glu), row3(d_skip)
    sgu_g3, sgu_b3 = row3(sgu_ln_g), row3(sgu_ln_b)
    ln1_g3, ln1_b3, ln2_g3, ln2_b3 = row3(ln1_g), row3(ln1_b), row3(ln2_g), row3(ln2_b)
    w_s32 = w_s.astype(F32)

    group_bias = [_group_rel_bias(rel_bias, g, dil) for g, (_, dil) in enumerate(ATT_PATTERNS)]
    expand = jnp.asarray(np.kron(np.eye(LANES, HEADS_PER_GROUP, dtype=np.float32),
                                 np.ones((1, HEAD_DIM), np.float32)))

    xf = x.reshape(t, d_model).astype(F32)
    xb = bf(xf)
    for l in range(depth):
        qkv = _matmul_bias(xb, w_in_b, b_in3, l, lambda j: j, 3 * QKV_WIDTH, BF16, 1024, bn)
        rest = _matmul_bias(xb, w_in_b, b_in3, l, rest_cols, rest_width, F32, 1024, bn)
        outs, lses = [], []
        for g, (window, dil) in enumerate(ATT_PATTERNS):
            o_g, lse_g = _attention_group(qkv, group_bias[g], g, dil, window // dil, bsz, seq)
            outs.append(o_g)
            lses.append(lse_g)
        yb = _spatial_gating(rest, zb_col, sgu_g3, sgu_b3, w_s32, b_s, l, 512)
        pw_re, pw_im, bb_re, bb_im = _s5_prep(lam_re[l], lam_im[l], log_dt[l], b_re[l], b_im[l])
        bmat, cmat, coef = _s5_matrices(pw_re, pw_im, bb_re, bb_im, c_re[l], c_im[l])
        yc = _s5(rest, uc_col, bmat, cmat, coef, d_skip3, l, bsz, seq, 256)
        merged = _merge(outs, lses, yb, yc, rest, gl_col, expand, w_pa_b, w_pb_b, w_pc_b, w_glu_b, b_glu3,
                        l, d_model, 256)
        xf, xb = _proj_ln(merged, xf, w_o_b, ln1_g3, ln1_b3, l, alpha, 256)
        h = _swiglu(xb, w_ffn_in_b, l, d_ff, 1024, 512)
        xf, xb = _ffn_out_ln(h, xf, w_ffn_out_b, ln2_g3, ln2_b3, l, alpha, 512, 512)
    return xf.reshape(bsz, seq, d_model).astype(x.dtype)
```

```python
import functools
import math

import numpy as np
import jax
import jax.numpy as jnp
from jax import lax
from jax.experimental import pallas as pl
from jax.experimental.pallas import tpu as pltpu

F32 = jnp.float32
BF16 = jnp.bfloat16

ATT_PATTERNS = ((128, 1), (512, 4), (2048, 16))
HEADS_PER_GROUP = 8
HEAD_DIM = 64
N_HEADS = len(ATT_PATTERNS) * HEADS_PER_GROUP
QKV_WIDTH = N_HEADS * HEAD_DIM
WIDTH_A = HEADS_PER_GROUP * HEAD_DIM
ATT_BLOCK = 128
N_REL_BUCKETS = 32
REL_MAX_DIST = 2048
NEG_INF = -1e30
CHUNK = 128
WIDTH_B = 768
N_GROUPS_B = 6
WIDTH_C = 768
SSM_GROUP = 16
N_GROUPS_C = WIDTH_C // SSM_GROUP
SSM_STATE = 64
N_BRANCH = 3
LN_EPS = 1e-5

LANES = 128
SUBLANES = 8
VMEM_LIMIT = 48 * 1024 * 1024

SLAB = LANES
GROUPS_PER_SLAB = SLAB // SSM_GROUP
N_SLABS = WIDTH_C // SLAB
SLAB_STATES = GROUPS_PER_SLAB * SSM_STATE


def _params(sem, vmem=VMEM_LIMIT):
    return pltpu.CompilerParams(dimension_semantics=sem, vmem_limit_bytes=vmem)


def _layer_norm_rows(y, g, b):
    mu = jnp.mean(y, axis=-1, keepdims=True)
    yc = y - mu
    var = jnp.mean(yc * yc, axis=-1, keepdims=True)
    return yc * lax.rsqrt(var + LN_EPS) * g + b


TOKEN_TILE = 2048


def _in_proj_kernel(x_ref, w_ref, b_ref, o_ref, *scratch, mode):
    acc = jnp.dot(x_ref[...], w_ref[...], preferred_element_type=F32) + b_ref[...]
    if mode == "plain":
        o_ref[...] = acc.astype(o_ref.dtype)
    elif mode == "sigmoid":
        o_ref[...] = jax.nn.sigmoid(acc).astype(o_ref.dtype)
    else:
        (scr_ref,) = scratch
        grp = pl.program_id(0) % len(ATT_PATTERNS)
        rows = o_ref.shape[0]
        for gi, (_, r) in enumerate(ATT_PATTERNS):
            @pl.when(grp == gi)
            def _(r=r):
                if r == 1:
                    o_ref[...] = acc.astype(o_ref.dtype)
                else:
                    per = rows // r
                    for kt in range(o_ref.shape[1] // LANES):
                        cs = slice(kt * LANES, (kt + 1) * LANES)
                        scr_ref[kt] = acc[:, cs]
                        for c in range(r):
                            o_ref[c * per:(c + 1) * per, cs] = (
                                scr_ref[kt, pl.ds(c, per, stride=r), :].astype(o_ref.dtype))


def _in_proj(x, w, b, layer, col_map, n_out, out_dtype, bm, bn, mode):
    m, k = x.shape
    scratch = [pltpu.VMEM((bn // LANES, bm, LANES), F32)] if mode == "streams" else []
    return pl.pallas_call(
        functools.partial(_in_proj_kernel, mode=mode),
        grid=(n_out // bn, m // bm),
        in_specs=[
            pl.BlockSpec((bm, k), lambda j, i: (i, 0)),
            pl.BlockSpec((None, k, bn), lambda j, i: (layer, 0, col_map(j))),
            pl.BlockSpec((None, 1, bn), lambda j, i: (layer, 0, col_map(j))),
        ],
        out_specs=pl.BlockSpec((bm, bn), lambda j, i: (i, j)),
        out_shape=jax.ShapeDtypeStruct((m, n_out), out_dtype),
        scratch_shapes=scratch,
        compiler_params=_params(("arbitrary", "arbitrary")),
        name="in_proj_" + mode,
    )(x, w, b)


def _attn_kernel(q_ref, k_ref, v_ref, kp_ref, vp_ref, bias_ref, o_ref, lse_ref, kf_ref, vf_ref,
                 *, nblk, n_steps, r):
    q_blk = ATT_BLOCK
    tile = pl.program_id(1)
    c = pl.program_id(2)
    kf_ref[0:q_blk, :] = kp_ref[...]
    kf_ref[q_blk:, :] = k_ref[...]
    vf_ref[0:q_blk, :] = vp_ref[...]
    vf_ref[q_blk:, :] = v_ref[...]
    row = lax.broadcasted_iota(jnp.int32, (q_blk, 2 * q_blk), 0)
    col = lax.broadcasted_iota(jnp.int32, (q_blk, 2 * q_blk), 1)
    steps = q_blk + row - col
    band = jnp.logical_and(steps >= 0, steps <= n_steps)
    lane = lax.broadcasted_iota(jnp.int32, (q_blk, LANES), 1)
    lo = lane < HEAD_DIM
    scale = HEAD_DIM ** -0.5

    def body(n, carry):
        r0 = pl.multiple_of(n * q_blk, q_blk)
        out_rows = pl.ds(r0, q_blk) if r == 1 else pl.ds(r0 * r + c, q_blk, stride=r)
        first_col = jnp.where(jnp.logical_and(tile == 0, n == 0), q_blk, 0)
        mask = jnp.logical_and(band, col >= first_col)
        lse_tile = jnp.zeros((q_blk, LANES), F32)
        for pr in range(HEADS_PER_GROUP // 2):
            cs = slice(pr * LANES, (pr + 1) * LANES)
            q2 = q_ref[pl.ds(r0, q_blk), cs]
            k2 = kf_ref[pl.ds(r0, 2 * q_blk), cs]
            v2 = vf_ref[pl.ds(r0, 2 * q_blk), cs]
            outs = []
            for hh in range(2):
                h = 2 * pr + hh
                sel = lo if hh == 0 else jnp.logical_not(lo)
                qm = jnp.where(sel, q2, jnp.zeros_like(q2))
                s = lax.dot_general(qm, k2, (((1,), (1,)), ((), ())), preferred_element_type=F32)
                s = jnp.where(mask, s * scale + bias_ref[h], NEG_INF)
                m = jnp.max(s, axis=-1, keepdims=True)
                p = jnp.exp(s - m)
                den = jnp.sum(p, axis=-1, keepdims=True)
                o = jnp.dot(p.astype(BF16), v2, preferred_element_type=F32) / den
                outs.append(o)
                lse_tile = jnp.where(lane == h, m + jnp.log(den), lse_tile)
            o_ref[pr, out_rows, :] = jnp.where(lo, outs[0], outs[1]).astype(o_ref.dtype)
        lse_ref[out_rows, :] = lse_tile
        return carry

    lax.fori_loop(0, nblk, body, 0)


def _attention_group(qkv, bias, g, dilation, n_steps, bsz, seq):
    r = dilation
    assert seq % TOKEN_TILE == 0 and TOKEN_TILE % (r * ATT_BLOCK) == 0
    ntile = seq // TOKEN_TILE
    per = TOKEN_TILE // r
    nblk = per // ATT_BLOCK
    gpw = QKV_WIDTH // WIDTH_A

    def cur(which):
        return lambda b, t, c: ((b * ntile + t) * r + c, which * gpw + g)

    def prev(which):
        return lambda b, t, c: (((b * ntile + jnp.maximum(t - 1, 0)) * r + c) * nblk + nblk - 1, which * gpw + g)

    return pl.pallas_call(
        functools.partial(_attn_kernel, nblk=nblk, n_steps=n_steps, r=r),
        grid=(bsz, ntile, r),
        in_specs=[
            pl.BlockSpec((per, WIDTH_A), cur(0)),
            pl.BlockSpec((per, WIDTH_A), cur(1)),
            pl.BlockSpec((per, WIDTH_A), cur(2)),
            pl.BlockSpec((ATT_BLOCK, WIDTH_A), prev(1)),
            pl.BlockSpec((ATT_BLOCK, WIDTH_A), prev(2)),
            pl.BlockSpec((HEADS_PER_GROUP, ATT_BLOCK, 2 * ATT_BLOCK), lambda b, t, c: (0, 0, 0)),
        ],
        out_specs=[
            pl.BlockSpec((WIDTH_A // LANES, TOKEN_TILE, LANES), lambda b, t, c: (0, b * ntile + t, 0)),
            pl.BlockSpec((TOKEN_TILE, LANES), lambda b, t, c: (b * ntile + t, 0)),
        ],
        out_shape=[
            jax.ShapeDtypeStruct((WIDTH_A // LANES, bsz * seq, LANES), F32),
            jax.ShapeDtypeStruct((bsz * seq, LANES), F32),
        ],
        scratch_shapes=[
            pltpu.VMEM((per + ATT_BLOCK, WIDTH_A), BF16),
            pltpu.VMEM((per + ATT_BLOCK, WIDTH_A), BF16),
        ],
        compiler_params=_params(("arbitrary", "arbitrary", "arbitrary")),
        name=f"attn_d{r}",
    )(qkv, qkv, qkv, qkv, qkv, bias)


def _t5_bucket(dist):
    max_exact = N_REL_BUCKETS // 2
    d = np.maximum(dist, 1).astype(np.float32)
    scale = (N_REL_BUCKETS - max_exact) / math.log(REL_MAX_DIST / max_exact)
    large = max_exact + (np.log(d / max_exact) * scale).astype(np.int32)
    large = np.minimum(large, N_REL_BUCKETS - 1)
    return np.where(dist < max_exact, dist, large).astype(np.int32)


def _group_rel_bias(rel_bias, g, dilation):
    i = np.arange(ATT_BLOCK)[:, None]
    kk = np.arange(2 * ATT_BLOCK)[None, :]
    steps = ATT_BLOCK + i - kk
    bucket = _t5_bucket(np.maximum(steps, 0) * dilation)
    cols = rel_bias[:, g * HEADS_PER_GROUP:(g + 1) * HEADS_PER_GROUP]
    return jnp.transpose(cols[bucket], (2, 0, 1)).astype(F32)


def _gmlp_kernel(z_ref, g_ref, b_ref, w_ref, bs_ref, o_ref, *, nchunk):
    z = jax.nn.gelu(z_ref[...])
    u = z[:, :WIDTH_B]
    v = _layer_norm_rows(z[:, WIDTH_B:], g_ref[...], b_ref[...]).astype(BF16)
    row = lax.broadcasted_iota(jnp.int32, (CHUNK, CHUNK), 0)
    col = lax.broadcasted_iota(jnp.int32, (CHUNK, CHUNK), 1)
    causal = row >= col
    bs = bs_ref[...]
    for g in range(N_GROUPS_B):
        cs = slice(g * LANES, (g + 1) * LANES)
        w = jnp.where(causal, w_ref[g], 0.0).astype(BF16)
        rhs = jnp.concatenate([v[n * CHUNK:(n + 1) * CHUNK, cs] for n in range(nchunk)], axis=1)
        mixed = jnp.dot(w, rhs, preferred_element_type=F32)
        for n in range(nchunk):
            rs = slice(n * CHUNK, (n + 1) * CHUNK)
            o_ref[rs, cs] = (u[rs, cs] * (mixed[:, n * LANES:(n + 1) * LANES] + bs[:, cs])).astype(o_ref.dtype)


def _spatial_gating(rest, zb_col, ln_g, ln_b, w_s, b_s, layer, tm):
    t = rest.shape[0]
    bs = jnp.repeat(jnp.transpose(b_s[layer]), LANES, axis=1)
    return pl.pallas_call(
        functools.partial(_gmlp_kernel, nchunk=tm // CHUNK),
        grid=(t // tm,),
        in_specs=[
            pl.BlockSpec((tm, 2 * WIDTH_B), lambda i: (i, zb_col // (2 * WIDTH_B))),
            pl.BlockSpec((None, 1, WIDTH_B), lambda i: (layer, 0, 0)),
            pl.BlockSpec((None, 1, WIDTH_B), lambda i: (layer, 0, 0)),
            pl.BlockSpec((None, N_GROUPS_B, CHUNK, CHUNK), lambda i: (layer, 0, 0, 0)),
            pl.BlockSpec((CHUNK, WIDTH_B), lambda i: (0, 0)),
        ],
        out_specs=pl.BlockSpec((tm, WIDTH_B), lambda i: (i, 0)),
        out_shape=jax.ShapeDtypeStruct((t, WIDTH_B), BF16),
        compiler_params=_params(("arbitrary",)),
        name="gmlp",
    )(rest, ln_g, ln_b, w_s, bs)


N_POW = LANES


def _s5_prep_kernel(lr_ref, li_ref, dt_ref, bre_ref, bim_ref, pre_ref, pim_ref, bbre_ref, bbim_ref):
    lr = lr_ref[...]
    li = li_ref[...]
    dt = jnp.exp(dt_ref[...])
    kpow = (lax.broadcasted_iota(jnp.int32, (1, N_POW), 1) + 1).astype(F32)
    mag = jnp.exp(lr * dt * kpow)
    ang = li * dt * kpow
    pre_ref[...] = mag * jnp.cos(ang)
    pim_ref[...] = mag * jnp.sin(ang)
    mag1 = jnp.exp(lr * dt)
    ab_re = mag1 * jnp.cos(li * dt)
    ab_im = mag1 * jnp.sin(li * dt)
    nrm = lr * lr + li * li
    cr = ((ab_re - 1.0) * lr + ab_im * li) / nrm
    ci = (ab_im * lr - (ab_re - 1.0) * li) / nrm
    bre = bre_ref[...]
    bim = bim_ref[...]
    bbre_ref[...] = cr * bre - ci * bim
    bbim_ref[...] = cr * bim + ci * bre


def _s5_prep(lam_re, lam_im, log_dt, b_re, b_im):
    gp = N_GROUPS_C * SSM_STATE
    col = lambda a: a.reshape(gp, 1).astype(F32)
    dt = jnp.repeat(log_dt.astype(F32), SSM_STATE).reshape(gp, 1)
    vm = lambda shp: pl.BlockSpec(shp, lambda: (0,) * len(shp))
    return pl.pallas_call(
        _s5_prep_kernel,
        in_specs=[vm((gp, 1)), vm((gp, 1)), vm((gp, 1)), vm((gp, SSM_GROUP)), vm((gp, SSM_GROUP))],
        out_specs=[vm((gp, N_POW)), vm((gp, N_POW)), vm((gp, SSM_GROUP)), vm((gp, SSM_GROUP))],
        out_shape=[jax.ShapeDtypeStruct((gp, N_POW), F32), jax.ShapeDtypeStruct((gp, N_POW), F32),
                   jax.ShapeDtypeStruct((gp, SSM_GROUP), F32), jax.ShapeDtypeStruct((gp, SSM_GROUP), F32)],
        name="s5_prep",
    )(col(lam_re), col(lam_im), dt, b_re.reshape(gp, SSM_GROUP).astype(F32), b_im.reshape(gp, SSM_GROUP).astype(F32))


def _s5_kernel(u_ref, bmat_ref, cmat_ref, coef_ref, d_ref, o_ref, carry_ref, *, rows):
    j = pl.program_id(2)

    @pl.when(j == 0)
    def _():
        carry_ref[...] = jnp.zeros_like(carry_ref)

    ns = SLAB_STATES
    nv = rows // SUBLANES
    u = u_ref[...]
    bu = jnp.dot(u.astype(BF16), bmat_ref[...], preferred_element_type=F32)
    xr = bu[:, :ns].reshape(nv, SUBLANES, ns)
    xi = bu[:, ns:].reshape(nv, SUBLANES, ns)
    for idx, k in enumerate((1, 2, 4)):
        cr = coef_ref[2 * idx]
        ci = coef_ref[2 * idx + 1]
        sr = pltpu.roll(xr, k, 1)
        si = pltpu.roll(xi, k, 1)
        xr, xi = xr + (cr * sr - ci * si), xi + (cr * si + ci * sr)
    pr = coef_ref[6]
    pi = coef_ref[7]
    c_r = carry_ref[0:1, :]
    c_i = carry_ref[1:2, :]
    out_r, out_i = [], []
    for r in range(nv):
        yr = xr[r] + (pr * c_r - pi * c_i)
        yi = xi[r] + (pr * c_i + pi * c_r)
        out_r.append(yr)
        out_i.append(yi)
        c_r = yr[SUBLANES - 1:SUBLANES, :]
        c_i = yi[SUBLANES - 1:SUBLANES, :]
    carry_ref[0:1, :] = c_r
    carry_ref[1:2, :] = c_i
    x = jnp.concatenate([jnp.concatenate(out_r, axis=0), jnp.concatenate(out_i, axis=0)], axis=1)
    y = jnp.dot(x.astype(BF16), cmat_ref[...], preferred_element_type=F32)
    o_ref[...] = jax.nn.gelu(y + d_ref[...] * u).astype(o_ref.dtype)


def _s5_matrices(pw_re, pw_im, bb_re, bb_im, c_re, c_im):
    eye = jnp.eye(GROUPS_PER_SLAB, dtype=F32)
    g8, p, h = GROUPS_PER_SLAB, SSM_STATE, SSM_GROUP
    bbr = bb_re.reshape(N_SLABS, g8, p, h)
    bbi = bb_im.reshape(N_SLABS, g8, p, h)

    def bdiag_in(bb):
        return jnp.einsum("sgph,gk->sghkp", bb, eye).reshape(N_SLABS, g8 * h, g8 * p)

    bmat = jnp.concatenate([bdiag_in(bbr), bdiag_in(bbi)], axis=2).astype(BF16)
    cr = c_re.astype(F32).reshape(N_SLABS, g8, h, p)
    ci = c_im.astype(F32).reshape(N_SLABS, g8, h, p)

    def bdiag_out(c):
        return jnp.einsum("sghp,gk->sgpkh", c, eye).reshape(N_SLABS, g8 * p, g8 * h)

    cmat = jnp.concatenate([bdiag_out(cr), -bdiag_out(ci)], axis=1).astype(BF16)
    pr = pw_re.reshape(N_SLABS, SLAB_STATES, N_POW)
    pi = pw_im.reshape(N_SLABS, SLAB_STATES, N_POW)
    rowi = jnp.arange(SUBLANES)[None, :, None]
    coefs = []
    for k in (1, 2, 4):
        for pw in (pr, pi):
            a = jnp.broadcast_to(pw[:, :, k - 1][:, None, :], (N_SLABS, SUBLANES, SLAB_STATES))
            coefs.append(jnp.where(rowi >= k, a, 0.0))
    for pw in (pr, pi):
        coefs.append(jnp.transpose(pw[:, :, :SUBLANES], (0, 2, 1)))
    return bmat, cmat, jnp.stack(coefs, axis=1)


def _s5(rest, uc_col, bmat, cmat, coef, d_skip, layer, bsz, seq, rows):
    rest3 = rest.reshape(bsz, seq, rest.shape[-1])
    c0 = uc_col // SLAB
    out = pl.pallas_call(
        functools.partial(_s5_kernel, rows=rows),
        grid=(bsz, N_SLABS, seq // rows),
        in_specs=[
            pl.BlockSpec((None, rows, SLAB), lambda b, s, j: (b, j, c0 + s)),
            pl.BlockSpec((None, SLAB, 2 * SLAB_STATES), lambda b, s, j: (s, 0, 0)),
            pl.BlockSpec((None, 2 * SLAB_STATES, SLAB), lambda b, s, j: (s, 0, 0)),
            pl.BlockSpec((None, 8, SUBLANES, SLAB_STATES), lambda b, s, j: (s, 0, 0, 0)),
            pl.BlockSpec((None, 1, SLAB), lambda b, s, j: (layer, 0, s)),
        ],
        out_specs=pl.BlockSpec((None, rows, SLAB), lambda b, s, j: (b, j, s)),
        out_shape=jax.ShapeDtypeStruct((bsz, seq, WIDTH_C), F32),
        scratch_shapes=[pltpu.VMEM((SUBLANES, SLAB_STATES), F32)],
        compiler_params=_params(("arbitrary", "arbitrary", "arbitrary")),
        name="s5_scan",
    )(rest3, bmat, cmat, coef, d_skip)
    return out.reshape(bsz * seq, WIDTH_C)


def _merge_kernel(o0_ref, o1_ref, o2_ref, l0_ref, l1_ref, l2_ref, yb_ref, yc_ref,
                  g0_ref, g1_ref, g2_ref, e_ref, wpa_ref, wpb_ref, wpc_ref, wglu_ref, bglu_ref, out_ref):
    l0, l1, l2 = l0_ref[...], l1_ref[...], l2_ref[...]
    m = jnp.maximum(jnp.maximum(l0, l1), l2)
    e0, e1, e2 = jnp.exp(l0 - m), jnp.exp(l1 - m), jnp.exp(l2 - m)
    den = e0 + e1 + e2
    e = e_ref[...]

    def expand(w):
        hi = w.astype(BF16)
        rem = w - hi.astype(F32)
        mid = rem.astype(BF16)
        lo = (rem - mid.astype(F32)).astype(BF16)
        return (jnp.dot(hi, e, preferred_element_type=F32) + jnp.dot(mid, e, preferred_element_type=F32)
                + jnp.dot(lo, e, preferred_element_type=F32))

    planes = lambda o_ref: jnp.concatenate([o_ref[p] for p in range(o_ref.shape[0])], axis=1)
    ya = (expand(e0 / den) * planes(o0_ref) + expand(e1 / den) * planes(o1_ref) + expand(e2 / den) * planes(o2_ref))
    pa = jnp.dot(ya.astype(BF16), wpa_ref[...], preferred_element_type=F32)
    pb = jnp.dot(yb_ref[...], wpb_ref[...], preferred_element_type=F32)
    yc = yc_ref[...]
    glu = yc * jax.nn.sigmoid(jnp.dot(yc.astype(BF16), wglu_ref[...], preferred_element_type=F32) + bglu_ref[...])
    pc = jnp.dot(glu.astype(BF16), wpc_ref[...], preferred_element_type=F32)
    merged = (g0_ref[...].astype(F32) * pa + g1_ref[...].astype(F32) * pb + g2_ref[...].astype(F32) * pc)
    out_ref[...] = merged.astype(out_ref.dtype)


def _resident(shape, index_map):
    return pl.BlockSpec(shape, index_map, pipeline_mode=pl.Buffered(1))


def _merge(outs, lses, yb, yc, gates, expand, w_pa, w_pb, w_pc, w_glu, b_glu, layer, d_model, tm):
    t = yb.shape[0]
    row = lambda w: pl.BlockSpec((tm, w), lambda i: (i, 0))
    full = lambda a, b: _resident((None, a, b), lambda i: (layer, 0, 0))
    return pl.pallas_call(
        _merge_kernel,
        grid=(t // tm,),
        in_specs=[pl.BlockSpec((WIDTH_A // LANES, tm, LANES), lambda i: (0, i, 0))] * 3
        + [row(LANES)] * 3 + [row(WIDTH_B), row(WIDTH_C)]
        + [pl.BlockSpec((tm, d_model), lambda i, k=k: (i, k)) for k in range(N_BRANCH)]
        + [_resident((LANES, WIDTH_A), lambda i: (0, 0)),
           full(WIDTH_A, d_model), full(WIDTH_B, d_model), full(WIDTH_C, d_model),
           full(WIDTH_C, WIDTH_C), full(1, WIDTH_C)],
        out_specs=pl.BlockSpec((tm, d_model), lambda i: (i, 0)),
        out_shape=jax.ShapeDtypeStruct((t, d_model), BF16),
        compiler_params=_params(("arbitrary",)),
        name="merge",
    )(*outs, *lses, yb, yc, gates, gates, gates, expand, w_pa, w_pb, w_pc, w_glu, b_glu)


def _proj_ln_kernel(m_ref, x_ref, w_ref, g_ref, b_ref, o_ref, ob_ref, *, alpha):
    f = jnp.dot(m_ref[...], w_ref[...], preferred_element_type=F32)
    y = _layer_norm_rows(alpha * x_ref[...] + f, g_ref[...], b_ref[...])
    o_ref[...] = y
    ob_ref[...] = y.astype(BF16)


def _proj_ln(merged, x, w_o, ln_g, ln_b, layer, alpha, tm):
    t, d = x.shape
    return pl.pallas_call(
        functools.partial(_proj_ln_kernel, alpha=alpha),
        grid=(t // tm,),
        in_specs=[
            pl.BlockSpec((tm, d), lambda i: (i, 0)),
            pl.BlockSpec((tm, d), lambda i: (i, 0)),
            _resident((None, d, d), lambda i: (layer, 0, 0)),
            _resident((None, 1, d), lambda i: (layer, 0, 0)),
            _resident((None, 1, d), lambda i: (layer, 0, 0)),
        ],
        out_specs=[pl.BlockSpec((tm, d), lambda i: (i, 0)), pl.BlockSpec((tm, d), lambda i: (i, 0))],
        out_shape=[jax.ShapeDtypeStruct((t, d), F32), jax.ShapeDtypeStruct((t, d), BF16)],
        compiler_params=_params(("arbitrary",)),
        name="out_proj_ln",
    )(merged, x, w_o, ln_g, ln_b)


def _swiglu_kernel(x_ref, wg_ref, wu_ref, o_ref):
    x = x_ref[...]
    gate = jnp.dot(x, wg_ref[...], preferred_element_type=F32)
    up = jnp.dot(x, wu_ref[...], preferred_element_type=F32)
    o_ref[...] = (jax.nn.silu(gate) * up).astype(o_ref.dtype)


def _swiglu(xb, w_in, layer, d_ff, bm, bn):
    t, d = xb.shape
    nj = d_ff // bn
    return pl.pallas_call(
        _swiglu_kernel,
        grid=(nj, t // bm),
        in_specs=[
            pl.BlockSpec((bm, d), lambda j, i: (i, 0)),
            pl.BlockSpec((None, d, bn), lambda j, i: (layer, 0, j)),
            pl.BlockSpec((None, d, bn), lambda j, i: (layer, 0, j + nj)),
        ],
        out_specs=pl.BlockSpec((bm, bn), lambda j, i: (i, j)),
        out_shape=jax.ShapeDtypeStruct((t, d_ff), BF16),
        compiler_params=_params(("arbitrary", "arbitrary")),
        name="ffn_in",
    )(xb, w_in, w_in)


def _ffn_out_ln_kernel(h_ref, w_ref, x_ref, g_ref, b_ref, o_ref, ob_ref, acc_ref, *, alpha, nk):
    k = pl.program_id(1)
    part = jnp.dot(h_ref[...], w_ref[...], preferred_element_type=F32)

    @pl.when(k == 0)
    def _():
        acc_ref[...] = part

    @pl.when(k > 0)
    def _():
        acc_ref[...] += part

    @pl.when(k == nk - 1)
    def _():
        y = _layer_norm_rows(alpha * x_ref[...] + acc_ref[...], g_ref[...], b_ref[...])
        o_ref[...] = y
        ob_ref[...] = y.astype(BF16)


def _ffn_out_ln(h, x, w_out, ln_g, ln_b, layer, alpha, bm, bk):
    t, d = x.shape
    d_ff = h.shape[1]
    nk = d_ff // bk
    return pl.pallas_call(
        functools.partial(_ffn_out_ln_kernel, alpha=alpha, nk=nk),
        grid=(t // bm, nk),
        in_specs=[
            pl.BlockSpec((bm, bk), lambda i, k: (i, k)),
            pl.BlockSpec((None, bk, d), lambda i, k: (layer, k, 0)),
            pl.BlockSpec((bm, d), lambda i, k: (i, 0)),
            pl.BlockSpec((None, 1, d), lambda i, k: (layer, 0, 0)),
            pl.BlockSpec((None, 1, d), lambda i, k: (layer, 0, 0)),
        ],
        out_specs=[pl.BlockSpec((bm, d), lambda i, k: (i, 0)), pl.BlockSpec((bm, d), lambda i, k: (i, 0))],
        out_shape=[jax.ShapeDtypeStruct((t, d), F32), jax.ShapeDtypeStruct((t, d), BF16)],
        scratch_shapes=[pltpu.VMEM((bm, d), F32)],
        compiler_params=_params(("arbitrary", "arbitrary")),
        name="ffn_out_ln",
    )(h, w_out, x, ln_g, ln_b)


@jax.jit
def kernel(x, w_in, b_in, rel_bias, sgu_ln_g, sgu_ln_b, w_s, b_s, lam_re, lam_im, log_dt, b_re, b_im, c_re, c_im, d_skip, w_glu, b_glu, w_pa, w_pb, w_pc, w_o, ln1_g, ln1_b, w_ffn_in, w_ffn_out, ln2_g, ln2_b):
    bsz, seq, d_model = x.shape
    depth = w_in.shape[0]
    d_ff = w_ffn_out.shape[1]
    t = bsz * seq
    alpha = (2 * depth) ** 0.25
    in_cols = w_in.shape[-1]
    gl_width = N_BRANCH * d_model
    zb_src = 3 * QKV_WIDTH
    uc_src = zb_src + 2 * WIDTH_B
    gl_src = uc_src + WIDTH_C
    assert gl_src + gl_width == in_cols
    bn = 768
    zu_width = 2 * WIDTH_B + WIDTH_C
    zb_col, uc_col = 0, 2 * WIDTH_B

    bf = lambda a: a.astype(BF16)
    row3 = lambda a: a.reshape(a.shape[0], 1, a.shape[1]).astype(F32)
    w_in_b, w_pa_b, w_pb_b, w_pc_b, w_glu_b = bf(w_in), bf(w_pa), bf(w_pb), bf(w_pc), bf(w_glu)
    w_o_b, w_ffn_in_b, w_ffn_out_b = bf(w_o), bf(w_ffn_in), bf(w_ffn_out)
    b_in3, b_glu3, d_skip3 = row3(b_in), row3(b_glu), row3(d_skip)
    sgu_g3, sgu_b3 = row3(sgu_ln_g), row3(sgu_ln_b)
    ln1_g3, ln1_b3, ln2_g3, ln2_b3 = row3(ln1_g), row3(ln1_b), row3(ln2_g), row3(ln2_b)
    w_s32 = w_s.astype(F32)

    group_bias = [_group_rel_bias(rel_bias, g, dil) for g, (_, dil) in enumerate(ATT_PATTERNS)]
    expand = jnp.asarray(np.kron(np.eye(LANES, HEADS_PER_GROUP, dtype=np.float32),
                                 np.ones((1, HEAD_DIM), np.float32)), dtype=BF16)

    xf = x.reshape(t, d_model).astype(F32)
    xb = bf(xf)
    for l in range(depth):
        qkv = _in_proj(xb, w_in_b, b_in3, l, lambda j: j, 3 * QKV_WIDTH, BF16, TOKEN_TILE, WIDTH_A, "streams")
        zu = _in_proj(xb, w_in_b, b_in3, l, lambda j: j + zb_src // bn, zu_width, F32, TOKEN_TILE, bn, "plain")
        gates = _in_proj(xb, w_in_b, b_in3, l, lambda j: j + gl_src // bn, gl_width, BF16, TOKEN_TILE, bn,
                         "sigmoid")
        outs, lses = [], []
        for g, (window, dil) in enumerate(ATT_PATTERNS):
            o_g, lse_g = _attention_group(qkv, group_bias[g], g, dil, window // dil, bsz, seq)
            outs.append(o_g)
            lses.append(lse_g)
        yb = _spatial_gating(zu, zb_col, sgu_g3, sgu_b3, w_s32, b_s, l, 512)
        pw_re, pw_im, bb_re, bb_im = _s5_prep(lam_re[l], lam_im[l], log_dt[l], b_re[l], b_im[l])
        bmat, cmat, coef = _s5_matrices(pw_re, pw_im, bb_re, bb_im, c_re[l], c_im[l])
        yc = _s5(zu, uc_col, bmat, cmat, coef, d_skip3, l, bsz, seq, 256)
        merged = _merge(outs, lses, yb, yc, gates, expand, w_pa_b, w_pb_b, w_pc_b, w_glu_b, b_glu3,
                        l, d_model, 512)
        xf, xb = _proj_ln(merged, xf, w_o_b, ln1_g3, ln1_b3, l, alpha, 512)
        h = _swiglu(xb, w_ffn_in_b, l, d_ff, 1024, 512)
        xf, xb = _ffn_out_ln(h, xf, w_ffn_out_b, ln2_g3, ln2_b3, l, alpha, 512, 1408)
    return xf.reshape(bsz, seq, d_model).astype(x.dtype)
```

```python
import functools
import math

import numpy as np
import jax
import jax.numpy as jnp
from jax import lax
from jax.experimental import pallas as pl
from jax.experimental.pallas import tpu as pltpu

F32 = jnp.float32
BF16 = jnp.bfloat16

ATT_PATTERNS = ((128, 1), (512, 4), (2048, 16))
HEADS_PER_GROUP = 8
HEAD_DIM = 64
N_HEADS = len(ATT_PATTERNS) * HEADS_PER_GROUP
QKV_WIDTH = N_HEADS * HEAD_DIM
WIDTH_A = HEADS_PER_GROUP * HEAD_DIM
ATT_BLOCK = 128
N_REL_BUCKETS = 32
REL_MAX_DIST = 2048
NEG_INF = -1e30
CHUNK = 128
WIDTH_B = 768
N_GROUPS_B = 6
WIDTH_C = 768
SSM_GROUP = 16
N_GROUPS_C = WIDTH_C // SSM_GROUP
SSM_STATE = 64
N_BRANCH = 3
LN_EPS = 1e-5

LANES = 128
SUBLANES = 8
VMEM_LIMIT = 48 * 1024 * 1024

SLAB = LANES
GROUPS_PER_SLAB = SLAB // SSM_GROUP
N_SLABS = WIDTH_C // SLAB
SLAB_STATES = GROUPS_PER_SLAB * SSM_STATE


def _params(sem, vmem=VMEM_LIMIT):
    return pltpu.CompilerParams(dimension_semantics=sem, vmem_limit_bytes=vmem)


def _layer_norm_rows(y, g, b):
    mu = jnp.mean(y, axis=-1, keepdims=True)
    yc = y - mu
    var = jnp.mean(yc * yc, axis=-1, keepdims=True)
    return yc * lax.rsqrt(var + LN_EPS) * g + b


TOKEN_TILE = 2048


def _in_proj_kernel(x_ref, w_ref, b_ref, o_ref, *scratch, mode):
    acc = jnp.dot(x_ref[...], w_ref[...], preferred_element_type=F32) + b_ref[...]
    if mode == "plain":
        o_ref[...] = acc.astype(o_ref.dtype)
    elif mode == "sigmoid":
        o_ref[...] = (0.5 * jnp.tanh(0.5 * acc) + 0.5).astype(o_ref.dtype)
    else:
        (scr_ref,) = scratch
        grp = pl.program_id(0) % len(ATT_PATTERNS)
        rows = o_ref.shape[0]
        for gi, (_, r) in enumerate(ATT_PATTERNS):
            @pl.when(grp == gi)
            def _(r=r):
                if r == 1:
                    o_ref[...] = acc.astype(o_ref.dtype)
                else:
                    per = rows // r
                    for kt in range(o_ref.shape[1] // LANES):
                        cs = slice(kt * LANES, (kt + 1) * LANES)
                        scr_ref[kt] = acc[:, cs]
                        for c in range(r):
                            o_ref[c * per:(c + 1) * per, cs] = (
                                scr_ref[kt, pl.ds(c, per, stride=r), :].astype(o_ref.dtype))


def _in_proj(x, w, b, layer, col_map, n_out, out_dtype, bm, bn, mode):
    m, k = x.shape
    scratch = [pltpu.VMEM((bn // LANES, bm, LANES), F32)] if mode == "streams" else []
    return pl.pallas_call(
        functools.partial(_in_proj_kernel, mode=mode),
        grid=(n_out // bn, m // bm),
        in_specs=[
            pl.BlockSpec((bm, k), lambda j, i: (i, 0)),
            pl.BlockSpec((None, k, bn), lambda j, i: (layer, 0, col_map(j))),
            pl.BlockSpec((None, 1, bn), lambda j, i: (layer, 0, col_map(j))),
        ],
        out_specs=pl.BlockSpec((bm, bn), lambda j, i: (i, j)),
        out_shape=jax.ShapeDtypeStruct((m, n_out), out_dtype),
        scratch_shapes=scratch,
        compiler_params=_params(("arbitrary", "arbitrary")),
        name="in_proj_" + mode,
    )(x, w, b)


def _attn_kernel(q_ref, k_ref, v_ref, kp_ref, vp_ref, bias_ref, o_ref, lse_ref, kf_ref, vf_ref,
                 *, nblk, r):
    q_blk = ATT_BLOCK
    tile = pl.program_id(1)
    c = pl.program_id(2)
    kf_ref[0:q_blk, :] = kp_ref[...]
    kf_ref[q_blk:, :] = k_ref[...]
    vf_ref[0:q_blk, :] = vp_ref[...]
    vf_ref[q_blk:, :] = v_ref[...]
    lane = lax.broadcasted_iota(jnp.int32, (q_blk, LANES), 1)
    lo = lane < HEAD_DIM
    scale = HEAD_DIM ** -0.5

    def body(n, carry):
        r0 = pl.multiple_of(n * q_blk, q_blk)
        out_rows = pl.ds(r0, q_blk) if r == 1 else pl.ds(r0 * r + c, q_blk, stride=r)
        tab = jnp.where(jnp.logical_and(tile == 0, n == 0), HEADS_PER_GROUP, 0)
        lse_tile = jnp.zeros((q_blk, LANES), F32)
        for pr in range(HEADS_PER_GROUP // 2):
            cs = slice(pr * LANES, (pr + 1) * LANES)
            q2 = q_ref[pl.ds(r0, q_blk), cs]
            k2 = kf_ref[pl.ds(r0, 2 * q_blk), cs]
            v2 = vf_ref[pl.ds(r0, 2 * q_blk), cs]
            outs = []
            for hh in range(2):
                h = 2 * pr + hh
                sel = lo if hh == 0 else jnp.logical_not(lo)
                qm = jnp.where(sel, q2 * scale, jnp.zeros_like(q2))
                s = lax.dot_general(qm, k2, (((1,), (1,)), ((), ())), preferred_element_type=F32)
                s = s + bias_ref[tab + h]
                m = jnp.max(s, axis=-1, keepdims=True)
                p = jnp.exp(s - m)
                den = jnp.sum(p, axis=-1, keepdims=True)
                o = jnp.dot(p.astype(BF16), v2, preferred_element_type=F32) / den
                outs.append(o)
                lse_tile = jnp.where(lane == h, m + jnp.log(den), lse_tile)
            o_ref[pr, out_rows, :] = jnp.where(lo, outs[0], outs[1]).astype(o_ref.dtype)
        lse_ref[out_rows, :] = lse_tile
        return carry

    lax.fori_loop(0, nblk, body, 0)


def _attention_group(qkv, bias, g, dilation, bsz, seq):
    r = dilation
    assert seq % TOKEN_TILE == 0 and TOKEN_TILE % (r * ATT_BLOCK) == 0
    ntile = seq // TOKEN_TILE
    per = TOKEN_TILE // r
    nblk = per // ATT_BLOCK
    gpw = QKV_WIDTH // WIDTH_A

    def cur(which):
        return lambda b, t, c: ((b * ntile + t) * r + c, which * gpw + g)

    def prev(which):
        return lambda b, t, c: (((b * ntile + jnp.maximum(t - 1, 0)) * r + c) * nblk + nblk - 1, which * gpw + g)

    return pl.pallas_call(
        functools.partial(_attn_kernel, nblk=nblk, r=r),
        grid=(bsz, ntile, r),
        in_specs=[
            pl.BlockSpec((per, WIDTH_A), cur(0)),
            pl.BlockSpec((per, WIDTH_A), cur(1)),
            pl.BlockSpec((per, WIDTH_A), cur(2)),
            pl.BlockSpec((ATT_BLOCK, WIDTH_A), prev(1)),
            pl.BlockSpec((ATT_BLOCK, WIDTH_A), prev(2)),
            _resident((2 * HEADS_PER_GROUP, ATT_BLOCK, 2 * ATT_BLOCK), lambda b, t, c: (0, 0, 0)),
        ],
        out_specs=[
            pl.BlockSpec((WIDTH_A // LANES, TOKEN_TILE, LANES), lambda b, t, c: (0, b * ntile + t, 0)),
            pl.BlockSpec((TOKEN_TILE, LANES), lambda b, t, c: (b * ntile + t, 0)),
        ],
        out_shape=[
            jax.ShapeDtypeStruct((WIDTH_A // LANES, bsz * seq, LANES), F32),
            jax.ShapeDtypeStruct((bsz * seq, LANES), F32),
        ],
        scratch_shapes=[
            pltpu.VMEM((per + ATT_BLOCK, WIDTH_A), BF16),
            pltpu.VMEM((per + ATT_BLOCK, WIDTH_A), BF16),
        ],
        compiler_params=_params(("arbitrary", "arbitrary", "arbitrary")),
        name=f"attn_d{r}",
    )(qkv, qkv, qkv, qkv, qkv, bias)


def _t5_bucket(dist):
    max_exact = N_REL_BUCKETS // 2
    d = np.maximum(dist, 1).astype(np.float32)
    scale = (N_REL_BUCKETS - max_exact) / math.log(REL_MAX_DIST / max_exact)
    large = max_exact + (np.log(d / max_exact) * scale).astype(np.int32)
    large = np.minimum(large, N_REL_BUCKETS - 1)
    return np.where(dist < max_exact, dist, large).astype(np.int32)


def _group_rel_bias(rel_bias, g, dilation, n_steps):
    i = np.arange(ATT_BLOCK)[:, None]
    kk = np.arange(2 * ATT_BLOCK)[None, :]
    steps = ATT_BLOCK + i - kk
    bucket = _t5_bucket(np.maximum(steps, 0) * dilation)
    cols = rel_bias[:, g * HEADS_PER_GROUP:(g + 1) * HEADS_PER_GROUP]
    bias = jnp.transpose(cols[bucket], (2, 0, 1)).astype(F32)
    band = (steps >= 0) & (steps <= n_steps)
    first = band & (kk >= ATT_BLOCK)
    return jnp.concatenate([jnp.where(band[None], bias, NEG_INF), jnp.where(first[None], bias, NEG_INF)], axis=0)


def _gmlp_kernel(z_ref, g_ref, b_ref, w_ref, bs_ref, o_ref, *, nchunk):
    z = jax.nn.gelu(z_ref[...])
    u = z[:, :WIDTH_B]
    v = _layer_norm_rows(z[:, WIDTH_B:], g_ref[...], b_ref[...]).astype(BF16)
    row = lax.broadcasted_iota(jnp.int32, (CHUNK, CHUNK), 0)
    col = lax.broadcasted_iota(jnp.int32, (CHUNK, CHUNK), 1)
    causal = row >= col
    bs = bs_ref[...]
    for g in range(N_GROUPS_B):
        cs = slice(g * LANES, (g + 1) * LANES)
        w = jnp.where(causal, w_ref[g], 0.0).astype(BF16)
        rhs = jnp.concatenate([v[n * CHUNK:(n + 1) * CHUNK, cs] for n in range(nchunk)], axis=1)
        mixed = jnp.dot(w, rhs, preferred_element_type=F32)
        for n in range(nchunk):
            rs = slice(n * CHUNK, (n + 1) * CHUNK)
            o_ref[rs, cs] = (u[rs, cs] * (mixed[:, n * LANES:(n + 1) * LANES] + bs[:, cs])).astype(o_ref.dtype)


def _spatial_gating(rest, zb_col, ln_g, ln_b, w_s, b_s, layer, tm):
    t = rest.shape[0]
    bs = jnp.repeat(jnp.transpose(b_s[layer]), LANES, axis=1)
    return pl.pallas_call(
        functools.partial(_gmlp_kernel, nchunk=tm // CHUNK),
        grid=(t // tm,),
        in_specs=[
            pl.BlockSpec((tm, 2 * WIDTH_B), lambda i: (i, zb_col // (2 * WIDTH_B))),
            pl.BlockSpec((None, 1, WIDTH_B), lambda i: (layer, 0, 0)),
            pl.BlockSpec((None, 1, WIDTH_B), lambda i: (layer, 0, 0)),
            pl.BlockSpec((None, N_GROUPS_B, CHUNK, CHUNK), lambda i: (layer, 0, 0, 0)),
            pl.BlockSpec((CHUNK, WIDTH_B), lambda i: (0, 0)),
        ],
        out_specs=pl.BlockSpec((tm, WIDTH_B), lambda i: (i, 0)),
        out_shape=jax.ShapeDtypeStruct((t, WIDTH_B), BF16),
        compiler_params=_params(("arbitrary",)),
        name="gmlp",
    )(rest, ln_g, ln_b, w_s, bs)


S5_CHUNK = 8
CHUNK_LANES = S5_CHUNK * SLAB
STATE_LANES = 2 * SLAB_STATES
GH_ROWS = N_GROUPS_C * SSM_GROUP


def _s5_prep_kernel(lr_ref, li_ref, dt_ref, bt_re_ref, bt_im_ref, c_re_ref, c_im_ref,
                    kt_ref, w_re_ref, w_im_ref, o_re_ref, o_im_ref, a_re_ref, a_im_ref):
    lr = lr_ref[...]
    li = li_ref[...]
    dt = jnp.exp(dt_ref[...])

    def apow(k):
        mag = jnp.exp(lr * dt * k)
        ang = li * dt * k
        return mag * jnp.cos(ang), mag * jnp.sin(ang)

    e = pl.program_id(1).astype(F32)
    ab_re, ab_im = apow(1.0)
    nrm = lr * lr + li * li
    zr = ((ab_re - 1.0) * lr + ab_im * li) / nrm
    zi = (ab_im * lr - (ab_re - 1.0) * li) / nrm
    bt_re, bt_im = bt_re_ref[...], bt_im_ref[...]
    c_re, c_im = c_re_ref[...], c_im_ref[...]
    contract_states = (((1,), (1,)), ((), ()))
    pr, pi = apow(e)
    dr = pr * zr - pi * zi
    di = pr * zi + pi * zr
    w_re_ref[...] = dr * bt_re - di * bt_im
    w_im_ref[...] = dr * bt_im + di * bt_re
    car = c_re * dr - c_im * di
    cai = c_re * di + c_im * dr
    kt_ref[...] = (lax.dot_general(car, bt_re, contract_states, precision=lax.Precision.HIGHEST,
                                   preferred_element_type=F32)
                   - lax.dot_general(cai, bt_im, contract_states, precision=lax.Precision.HIGHEST,
                                     preferred_element_type=F32))
    qr, qi = apow(e + 1.0)
    o_re_ref[...] = c_re * qr - c_im * qi
    o_im_ref[...] = c_re * qi + c_im * qr
    ar, ai = apow(S5_CHUNK * (e + 1.0))
    a_re_ref[...] = ar
    a_im_ref[...] = ai


def _s5_prep(lam_re, lam_im, log_dt, b_re, b_im, c_re, c_im):
    depth = lam_re.shape[0]
    rep = lambda a: jnp.repeat(a.astype(F32), SSM_GROUP, axis=1)
    lr, li = rep(lam_re), rep(lam_im)
    dt = rep(log_dt[..., None])
    bt = lambda b: jnp.transpose(b.astype(F32), (0, 1, 3, 2)).reshape(depth, GH_ROWS, SSM_STATE)
    cc = lambda c: c.astype(F32).reshape(depth, GH_ROWS, SSM_STATE)
    assert SUBLANES == S5_CHUNK
    tab = lambda w: pl.BlockSpec((None, None, GH_ROWS, w), lambda d, e: (d, e, 0, 0))
    vec = lambda w: pl.BlockSpec((None, GH_ROWS, w), lambda d, e: (d, 0, 0))
    sds = lambda w: jax.ShapeDtypeStruct((depth, S5_CHUNK, GH_ROWS, w), F32)
    return pl.pallas_call(
        _s5_prep_kernel,
        grid=(depth, S5_CHUNK),
        in_specs=[vec(SSM_STATE), vec(SSM_STATE), vec(1)] + [vec(SSM_STATE)] * 4,
        out_specs=[tab(GH_ROWS)] + [tab(SSM_STATE)] * 6,
        out_shape=[sds(GH_ROWS)] + [sds(SSM_STATE)] * 6,
        compiler_params=_params(("arbitrary", "arbitrary")),
        name="s5_prep",
    )(lr, li, dt, bt(b_re), bt(b_im), cc(c_re), cc(c_im))


def _s5_matrices(kt, w_re, w_im, o_re, o_im, a_re, a_im):
    depth = kt.shape[0]
    nl, g8, h, p = S5_CHUNK, GROUPS_PER_SLAB, SSM_GROUP, SSM_STATE
    eye = jnp.eye(g8, dtype=F32)
    k = jnp.einsum("dtghgk->dgthk", kt.reshape(depth, nl, N_GROUPS_C, h, N_GROUPS_C, h))
    k = k.reshape(depth, N_SLABS, g8, nl, h, h)
    lag = np.arange(nl)[None, :] - np.arange(nl)[:, None]
    toep = jnp.where((lag >= 0)[None, None, None, :, :, None, None], k[:, :, :, np.maximum(lag, 0)], 0.0)
    m = jnp.einsum("dsgjthk,gq->dsjgktqh", toep, eye).reshape(depth, N_SLABS, CHUNK_LANES, CHUNK_LANES)

    def out_rows(o):
        o = o.reshape(depth, nl, N_SLABS, g8, h, p)
        return jnp.einsum("dtsghp,gq->dsgptqh", o, eye).reshape(depth, N_SLABS, SLAB_STATES, CHUNK_LANES)

    mo = jnp.concatenate([m, out_rows(o_re), -out_rows(o_im)], axis=2).astype(BF16)

    def in_cols(w):
        w = w[:, ::-1].reshape(depth, nl, N_SLABS, g8, h, p)
        return jnp.einsum("djsgkp,gq->dsjgkqp", w, eye).reshape(depth, N_SLABS, CHUNK_LANES, SLAB_STATES)

    wmat = jnp.concatenate([in_cols(w_re), in_cols(w_im)], axis=3).astype(BF16)

    def per_state(a):
        a = a[:, :, ::h, :].reshape(depth, SUBLANES, N_SLABS, SLAB_STATES)
        return jnp.transpose(a, (0, 2, 1, 3))

    ar, ai = per_state(a_re), per_state(a_im)
    rowi = jnp.arange(SUBLANES)[None, None, :, None]
    coefs = []
    for kk in (1, 2, 4):
        for a in (ar, ai):
            coefs.append(jnp.where(rowi >= kk, a[:, :, kk - 1:kk, :], 0.0))
    coefs += [ar, ai]
    return wmat, mo, jnp.stack(coefs, axis=2)


def _s5_kernel(u_ref, w_ref, mo_ref, coef_ref, d_ref, o_ref, ubuf_ref):
    nl = S5_CHUNK
    seq = u_ref.shape[0]
    n = seq // nl
    nv = n // SUBLANES
    ns = SLAB_STATES
    ubuf_ref[0:nl, :] = jnp.zeros((nl, SLAB), F32)
    ubuf_ref[nl:, :] = u_ref[...]
    cur = [ubuf_ref[pl.ds(nl + j, n, stride=nl), :] for j in range(nl)]
    prev = [ubuf_ref[pl.ds(j, n, stride=nl), :] for j in range(nl)]
    u2 = jnp.concatenate([t.astype(BF16) for t in cur], axis=1)
    u2p = jnp.concatenate([t.astype(BF16) for t in prev], axis=1)
    sc = jnp.dot(u2p, w_ref[...], preferred_element_type=F32)
    xr = sc[:, :ns].reshape(nv, SUBLANES, ns)
    xi = sc[:, ns:].reshape(nv, SUBLANES, ns)
    for idx, k in enumerate((1, 2, 4)):
        cr = coef_ref[2 * idx]
        ci = coef_ref[2 * idx + 1]
        sr = pltpu.roll(xr, k, 1)
        si = pltpu.roll(xi, k, 1)
        xr, xi = xr + (cr * sr - ci * si), xi + (cr * si + ci * sr)
    pr = coef_ref[6]
    pi = coef_ref[7]
    c_r = jnp.zeros((1, ns), F32)
    c_i = jnp.zeros((1, ns), F32)
    out_r, out_i = [], []
    for r in range(nv):
        yr = xr[r] + (pr * c_r - pi * c_i)
        yi = xi[r] + (pr * c_i + pi * c_r)
        out_r.append(yr)
        out_i.append(yi)
        c_r = yr[SUBLANES - 1:SUBLANES, :]
        c_i = yi[SUBLANES - 1:SUBLANES, :]
    x = jnp.concatenate([jnp.concatenate(out_r, axis=0), jnp.concatenate(out_i, axis=0)], axis=1)
    lhs = jnp.concatenate([u2, x.astype(BF16)], axis=1)
    y = jnp.dot(lhs, mo_ref[...], preferred_element_type=F32)
    d = d_ref[...]
    for j in range(nl):
        o_ref[pl.ds(j, n, stride=nl), :] = jax.nn.gelu(y[:, j * SLAB:(j + 1) * SLAB] + d * cur[j]).astype(o_ref.dtype)


def _s5(zu, uc_col, wmat, mo, coef, d_skip, layer, bsz, seq):
    assert seq % (S5_CHUNK * SUBLANES) == 0
    zu3 = zu.reshape(bsz, seq, zu.shape[-1])
    c0 = uc_col // SLAB
    out = pl.pallas_call(
        _s5_kernel,
        grid=(N_SLABS, bsz),
        in_specs=[
            pl.BlockSpec((None, seq, SLAB), lambda s, b: (b, 0, c0 + s)),
            pl.BlockSpec((None, None, CHUNK_LANES, STATE_LANES), lambda s, b: (layer, s, 0, 0)),
            pl.BlockSpec((None, None, CHUNK_LANES + STATE_LANES, CHUNK_LANES), lambda s, b: (layer, s, 0, 0)),
            pl.BlockSpec((None, None, 8, SUBLANES, SLAB_STATES), lambda s, b: (layer, s, 0, 0, 0)),
            pl.BlockSpec((None, 1, SLAB), lambda s, b: (layer, 0, s)),
        ],
        out_specs=pl.BlockSpec((None, seq, SLAB), lambda s, b: (b, 0, s)),
        out_shape=jax.ShapeDtypeStruct((bsz, seq, WIDTH_C), F32),
        scratch_shapes=[pltpu.VMEM((seq + S5_CHUNK, SLAB), F32)],
        compiler_params=_params(("arbitrary", "arbitrary")),
        name="s5_scan",
    )(zu3, wmat, mo, coef, d_skip)
    return out.reshape(bsz * seq, WIDTH_C)


def _merge_kernel(o0_ref, o1_ref, o2_ref, l0_ref, l1_ref, l2_ref, yb_ref, yc_ref,
                  g0_ref, g1_ref, g2_ref, e_ref, wpa_ref, wpb_ref, wpc_ref, wglu_ref, bglu_ref, out_ref):
    l0, l1, l2 = l0_ref[...], l1_ref[...], l2_ref[...]
    m = jnp.maximum(jnp.maximum(l0, l1), l2)
    e0, e1, e2 = jnp.exp(l0 - m), jnp.exp(l1 - m), jnp.exp(l2 - m)
    den = e0 + e1 + e2
    e = e_ref[...]

    def expand(w):
        hi = w.astype(BF16)
        rem = w - hi.astype(F32)
        mid = rem.astype(BF16)
        lo = (rem - mid.astype(F32)).astype(BF16)
        return (jnp.dot(hi, e, preferred_element_type=F32) + jnp.dot(mid, e, preferred_element_type=F32)
                + jnp.dot(lo, e, preferred_element_type=F32))

    planes = lambda o_ref: jnp.concatenate([o_ref[p] for p in range(o_ref.shape[0])], axis=1)
    ya = (expand(e0 / den) * planes(o0_ref) + expand(e1 / den) * planes(o1_ref) + expand(e2 / den) * planes(o2_ref))
    pa = jnp.dot(ya.astype(BF16), wpa_ref[...], preferred_element_type=F32)
    pb = jnp.dot(yb_ref[...], wpb_ref[...], preferred_element_type=F32)
    yc = yc_ref[...]
    glu = yc * jax.nn.sigmoid(jnp.dot(yc.astype(BF16), wglu_ref[...], preferred_element_type=F32) + bglu_ref[...])
    pc = jnp.dot(glu.astype(BF16), wpc_ref[...], preferred_element_type=F32)
    merged = (g0_ref[...].astype(F32) * pa + g1_ref[...].astype(F32) * pb + g2_ref[...].astype(F32) * pc)
    out_ref[...] = merged.astype(out_ref.dtype)


def _resident(shape, index_map):
    return pl.BlockSpec(shape, index_map, pipeline_mode=pl.Buffered(1))


def _merge(outs, lses, yb, yc, gates, expand, w_pa, w_pb, w_pc, w_glu, b_glu, layer, d_model, tm):
    t = yb.shape[0]
    row = lambda w: pl.BlockSpec((tm, w), lambda i: (i, 0))
    full = lambda a, b: _resident((None, a, b), lambda i: (layer, 0, 0))
    return pl.pallas_call(
        _merge_kernel,
        grid=(t // tm,),
        in_specs=[pl.BlockSpec((WIDTH_A // LANES, tm, LANES), lambda i: (0, i, 0))] * 3
        + [row(LANES)] * 3 + [row(WIDTH_B), row(WIDTH_C)]
        + [pl.BlockSpec((tm, d_model), lambda i, k=k: (i, k)) for k in range(N_BRANCH)]
        + [_resident((LANES, WIDTH_A), lambda i: (0, 0)),
           full(WIDTH_A, d_model), full(WIDTH_B, d_model), full(WIDTH_C, d_model),
           full(WIDTH_C, WIDTH_C), full(1, WIDTH_C)],
        out_specs=pl.BlockSpec((tm, d_model), lambda i: (i, 0)),
        out_shape=jax.ShapeDtypeStruct((t, d_model), BF16),
        compiler_params=_params(("arbitrary",)),
        name="merge",
    )(*outs, *lses, yb, yc, gates, gates, gates, expand, w_pa, w_pb, w_pc, w_glu, b_glu)


def _proj_ln_kernel(m_ref, x_ref, w_ref, g_ref, b_ref, o_ref, ob_ref, *, alpha):
    f = jnp.dot(m_ref[...], w_ref[...], preferred_element_type=F32)
    y = _layer_norm_rows(alpha * x_ref[...] + f, g_ref[...], b_ref[...])
    o_ref[...] = y
    ob_ref[...] = y.astype(BF16)


def _proj_ln(merged, x, w_o, ln_g, ln_b, layer, alpha, tm):
    t, d = x.shape
    return pl.pallas_call(
        functools.partial(_proj_ln_kernel, alpha=alpha),
        grid=(t // tm,),
        in_specs=[
            pl.BlockSpec((tm, d), lambda i: (i, 0)),
            pl.BlockSpec((tm, d), lambda i: (i, 0)),
            _resident((None, d, d), lambda i: (layer, 0, 0)),
            _resident((None, 1, d), lambda i: (layer, 0, 0)),
            _resident((None, 1, d), lambda i: (layer, 0, 0)),
        ],
        out_specs=[pl.BlockSpec((tm, d), lambda i: (i, 0)), pl.BlockSpec((tm, d), lambda i: (i, 0))],
        out_shape=[jax.ShapeDtypeStruct((t, d), F32), jax.ShapeDtypeStruct((t, d), BF16)],
        compiler_params=_params(("arbitrary",)),
        name="out_proj_ln",
    )(merged, x, w_o, ln_g, ln_b)


def _swiglu_kernel(x_ref, wg_ref, wu_ref, o_ref):
    x = x_ref[...]
    gate = jnp.dot(x, wg_ref[...], preferred_element_type=F32)
    up = jnp.dot(x, wu_ref[...], preferred_element_type=F32)
    o_ref[...] = (jax.nn.silu(gate) * up).astype(o_ref.dtype)


def _swiglu(xb, w_in, layer, d_ff, bm, bn):
    t, d = xb.shape
    nj = d_ff // bn
    return pl.pallas_call(
        _swiglu_kernel,
        grid=(nj, t // bm),
        in_specs=[
            pl.BlockSpec((bm, d), lambda j, i: (i, 0)),
            pl.BlockSpec((None, d, bn), lambda j, i: (layer, 0, j)),
            pl.BlockSpec((None, d, bn), lambda j, i: (layer, 0, j + nj)),
        ],
        out_specs=pl.BlockSpec((bm, bn), lambda j, i: (i, j)),
        out_shape=jax.ShapeDtypeStruct((t, d_ff), BF16),
        compiler_params=_params(("arbitrary", "arbitrary")),
        name="ffn_in",
    )(xb, w_in, w_in)


def _ffn_out_ln_kernel(h_ref, w_ref, x_ref, g_ref, b_ref, o_ref, ob_ref, acc_ref, *, alpha, nk):
    k = pl.program_id(1)
    part = jnp.dot(h_ref[...], w_ref[...], preferred_element_type=F32)

    @pl.when(k == 0)
    def _():
        acc_ref[...] = part

    @pl.when(k > 0)
    def _():
        acc_ref[...] += part

    @pl.when(k == nk - 1)
    def _():
        y = _layer_norm_rows(alpha * x_ref[...] + acc_ref[...], g_ref[...], b_ref[...])
        o_ref[...] = y
        ob_ref[...] = y.astype(BF16)


def _ffn_out_ln(h, x, w_out, ln_g, ln_b, layer, alpha, bm, bk):
    t, d = x.shape
    d_ff = h.shape[1]
    nk = d_ff // bk
    return pl.pallas_call(
        functools.partial(_ffn_out_ln_kernel, alpha=alpha, nk=nk),
        grid=(t // bm, nk),
        in_specs=[
            pl.BlockSpec((bm, bk), lambda i, k: (i, k)),
            pl.BlockSpec((None, bk, d), lambda i, k: (layer, k, 0)),
            pl.BlockSpec((bm, d), lambda i, k: (i, 0)),
            pl.BlockSpec((None, 1, d), lambda i, k: (layer, 0, 0)),
            pl.BlockSpec((None, 1, d), lambda i, k: (layer, 0, 0)),
        ],
        out_specs=[pl.BlockSpec((bm, d), lambda i, k: (i, 0)), pl.BlockSpec((bm, d), lambda i, k: (i, 0))],
        out_shape=[jax.ShapeDtypeStruct((t, d), F32), jax.ShapeDtypeStruct((t, d), BF16)],
        scratch_shapes=[pltpu.VMEM((bm, d), F32)],
        compiler_params=_params(("arbitrary", "arbitrary")),
        name="ffn_out_ln",
    )(h, w_out, x, ln_g, ln_b)


@jax.jit
def kernel(x, w_in, b_in, rel_bias, sgu_ln_g, sgu_ln_b, w_s, b_s, lam_re, lam_im, log_dt, b_re, b_im, c_re, c_im, d_skip, w_glu, b_glu, w_pa, w_pb, w_pc, w_o, ln1_g, ln1_b, w_ffn_in, w_ffn_out, ln2_g, ln2_b):
    bsz, seq, d_model = x.shape
    depth = w_in.shape[0]
    d_ff = w_ffn_out.shape[1]
    t = bsz * seq
    alpha = (2 * depth) ** 0.25
    in_cols = w_in.shape[-1]
    gl_width = N_BRANCH * d_model
    zb_src = 3 * QKV_WIDTH
    uc_src = zb_src + 2 * WIDTH_B
    gl_src = uc_src + WIDTH_C
    assert gl_src + gl_width == in_cols
    bn = 768
    zu_width = 2 * WIDTH_B + WIDTH_C
    zb_col, uc_col = 0, 2 * WIDTH_B

    bf = lambda a: a.astype(BF16)
    row3 = lambda a: a.reshape(a.shape[0], 1, a.shape[1]).astype(F32)
    w_in_b, w_pa_b, w_pb_b, w_pc_b, w_glu_b = bf(w_in), bf(w_pa), bf(w_pb), bf(w_pc), bf(w_glu)
    w_o_b, w_ffn_in_b, w_ffn_out_b = bf(w_o), bf(w_ffn_in), bf(w_ffn_out)
    b_in3, b_glu3, d_skip3 = row3(b_in), row3(b_glu), row3(d_skip)
    sgu_g3, sgu_b3 = row3(sgu_ln_g), row3(sgu_ln_b)
    ln1_g3, ln1_b3, ln2_g3, ln2_b3 = row3(ln1_g), row3(ln1_b), row3(ln2_g), row3(ln2_b)
    w_s32 = w_s.astype(F32)

    group_bias = [_group_rel_bias(rel_bias, g, dil, win // dil) for g, (win, dil) in enumerate(ATT_PATTERNS)]
    expand = jnp.asarray(np.kron(np.eye(LANES, HEADS_PER_GROUP, dtype=np.float32),
                                 np.ones((1, HEAD_DIM), np.float32)), dtype=BF16)

    s5_w, s5_mo, s5_coef = _s5_matrices(*_s5_prep(lam_re, lam_im, log_dt, b_re, b_im, c_re, c_im))

    xf = x.reshape(t, d_model).astype(F32)
    xb = bf(xf)
    for l in range(depth):
        qkv = _in_proj(xb, w_in_b, b_in3, l, lambda j: j, 3 * QKV_WIDTH, BF16, TOKEN_TILE, WIDTH_A, "streams")
        zu = _in_proj(xb, w_in_b, b_in3, l, lambda j: j + zb_src // bn, zu_width, F32, TOKEN_TILE, bn, "plain")
        gates = _in_proj(xb, w_in_b, b_in3, l, lambda j: j + gl_src // bn, gl_width, BF16, TOKEN_TILE, bn,
                         "sigmoid")
        outs, lses = [], []
        for g, (window, dil) in enumerate(ATT_PATTERNS):
            o_g, lse_g = _attention_group(qkv, group_bias[g], g, dil, bsz, seq)
            outs.append(o_g)
            lses.append(lse_g)
        yb = _spatial_gating(zu, zb_col, sgu_g3, sgu_b3, w_s32, b_s, l, 512)
        yc = _s5(zu, uc_col, s5_w, s5_mo, s5_coef, d_skip3, l, bsz, seq)
        merged = _merge(outs, lses, yb, yc, gates, expand, w_pa_b, w_pb_b, w_pc_b, w_glu_b, b_glu3,
                        l, d_model, 512)
        xf, xb = _proj_ln(merged, xf, w_o_b, ln1_g3, ln1_b3, l, alpha, 512)
        h = _swiglu(xb, w_ffn_in_b, l, d_ff, 1024, 512)
        xf, xb = _ffn_out_ln(h, xf, w_ffn_out_b, ln2_g3, ln2_b3, l, alpha, 512, 1408)
    return xf.reshape(bsz, seq, d_model).astype(x.dtype)
```

```python
import functools
import math

import numpy as np
import jax
import jax.numpy as jnp
from jax import lax
from jax.experimental import pallas as pl
from jax.experimental.pallas import tpu as pltpu

F32 = jnp.float32
BF16 = jnp.bfloat16

ATT_PATTERNS = ((128, 1), (512, 4), (2048, 16))
HEADS_PER_GROUP = 8
HEAD_DIM = 64
N_HEADS = len(ATT_PATTERNS) * HEADS_PER_GROUP
QKV_WIDTH = N_HEADS * HEAD_DIM
WIDTH_A = HEADS_PER_GROUP * HEAD_DIM
ATT_BLOCK = 128
N_REL_BUCKETS = 32
REL_MAX_DIST = 2048
NEG_INF = -1e30
CHUNK = 128
WIDTH_B = 768
N_GROUPS_B = 6
WIDTH_C = 768
SSM_GROUP = 16
N_GROUPS_C = WIDTH_C // SSM_GROUP
SSM_STATE = 64
N_BRANCH = 3
LN_EPS = 1e-5

LANES = 128
SUBLANES = 8
VMEM_LIMIT = 48 * 1024 * 1024
VMEM_LIMIT_LARGE = 56 * 1024 * 1024

SLAB = LANES
GROUPS_PER_SLAB = SLAB // SSM_GROUP
N_SLABS = WIDTH_C // SLAB
SLAB_STATES = GROUPS_PER_SLAB * SSM_STATE


def _params(sem, vmem=VMEM_LIMIT):
    return pltpu.CompilerParams(dimension_semantics=sem, vmem_limit_bytes=vmem)


def _layer_norm_rows(y, g, b):
    mu = jnp.mean(y, axis=-1, keepdims=True)
    yc = y - mu
    var = jnp.mean(yc * yc, axis=-1, keepdims=True)
    return yc * lax.rsqrt(var + LN_EPS) * g + b


TOKEN_TILE = 2048


def _in_proj_kernel(x_ref, w_ref, b_ref, o_ref, wb_ref, *scratch, mode):
    @pl.when(pl.program_id(1) == 0)
    def _():
        wb_ref[...] = w_ref[...].astype(BF16)

    acc = jnp.dot(x_ref[...], wb_ref[...], preferred_element_type=F32) + b_ref[...]
    if mode == "plain":
        o_ref[...] = acc.astype(o_ref.dtype)
    elif mode == "sigmoid":
        o_ref[...] = (0.5 * jnp.tanh(0.5 * acc) + 0.5).astype(o_ref.dtype)
    else:
        (scr_ref,) = scratch
        grp = pl.program_id(0) % len(ATT_PATTERNS)
        rows = o_ref.shape[0]
        for gi, (_, r) in enumerate(ATT_PATTERNS):
            @pl.when(grp == gi)
            def _(r=r):
                if r == 1:
                    o_ref[...] = acc.astype(o_ref.dtype)
                else:
                    per = rows // r
                    for kt in range(o_ref.shape[1] // LANES):
                        cs = slice(kt * LANES, (kt + 1) * LANES)
                        scr_ref[kt] = acc[:, cs]
                        for c in range(r):
                            o_ref[c * per:(c + 1) * per, cs] = (
                                scr_ref[kt, pl.ds(c, per, stride=r), :].astype(o_ref.dtype))


def _in_proj(x, w, b, layer, col_map, n_out, out_dtype, bm, bn, mode):
    m, k = x.shape
    scratch = [pltpu.VMEM((k, bn), BF16)]
    if mode == "streams":
        scratch.append(pltpu.VMEM((bn // LANES, bm, LANES), F32))
    return pl.pallas_call(
        functools.partial(_in_proj_kernel, mode=mode),
        grid=(n_out // bn, m // bm),
        in_specs=[
            pl.BlockSpec((bm, k), lambda j, i: (i, 0)),
            pl.BlockSpec((None, k, bn), lambda j, i: (layer, 0, col_map(j))),
            pl.BlockSpec((None, 1, bn), lambda j, i: (layer, 0, col_map(j))),
        ],
        out_specs=pl.BlockSpec((bm, bn), lambda j, i: (i, j)),
        out_shape=jax.ShapeDtypeStruct((m, n_out), out_dtype),
        scratch_shapes=scratch,
        compiler_params=_params(("arbitrary", "arbitrary"), VMEM_LIMIT_LARGE),
        name="in_proj_" + mode,
    )(x, w, b)


def _attn_kernel(q_ref, k_ref, v_ref, kp_ref, vp_ref, base_ref, o_ref, lse_ref, kf_ref, vf_ref, bias_ref,
                 *, nblk, r):
    q_blk = ATT_BLOCK
    tile = pl.program_id(1)
    c = pl.program_id(2)

    @pl.when(jnp.logical_and(pl.program_id(0) == 0, jnp.logical_and(tile == 0, c == 0)))
    def _():
        col = lax.broadcasted_iota(jnp.int32, (q_blk, 2 * q_blk), 1)
        for h in range(HEADS_PER_GROUP):
            base = jnp.broadcast_to(base_ref[h:h + 1, :], (q_blk, 2 * q_blk))
            tab_h = pltpu.roll(base, 0, 1, stride=1, stride_axis=0)
            bias_ref[h] = tab_h
            bias_ref[HEADS_PER_GROUP + h] = jnp.where(col >= q_blk, tab_h, NEG_INF)

    kf_ref[0:q_blk, :] = kp_ref[...]
    kf_ref[q_blk:, :] = k_ref[...]
    vf_ref[0:q_blk, :] = vp_ref[...]
    vf_ref[q_blk:, :] = v_ref[...]
    lane = lax.broadcasted_iota(jnp.int32, (q_blk, LANES), 1)
    lo = lane < HEAD_DIM
    scale = HEAD_DIM ** -0.5

    def body(n, carry):
        r0 = pl.multiple_of(n * q_blk, q_blk)
        out_rows = pl.ds(r0, q_blk) if r == 1 else pl.ds(r0 * r + c, q_blk, stride=r)
        tab = jnp.where(jnp.logical_and(tile == 0, n == 0), HEADS_PER_GROUP, 0)
        lse_tile = jnp.zeros((q_blk, LANES), F32)
        for pr in range(HEADS_PER_GROUP // 2):
            cs = slice(pr * LANES, (pr + 1) * LANES)
            q2 = q_ref[pl.ds(r0, q_blk), cs]
            k2 = kf_ref[pl.ds(r0, 2 * q_blk), cs]
            v2 = vf_ref[pl.ds(r0, 2 * q_blk), cs]
            outs = []
            for hh in range(2):
                h = 2 * pr + hh
                sel = lo if hh == 0 else jnp.logical_not(lo)
                qm = jnp.where(sel, q2 * scale, jnp.zeros_like(q2))
                s = lax.dot_general(qm, k2, (((1,), (1,)), ((), ())), preferred_element_type=F32)
                s = s + bias_ref[tab + h]
                m = jnp.max(s, axis=-1, keepdims=True)
                p = jnp.exp(s - m)
                den = jnp.sum(p, axis=-1, keepdims=True)
                o = jnp.dot(p.astype(BF16), v2, preferred_element_type=F32) / den
                outs.append(o)
                lse_tile = jnp.where(lane == h, m + jnp.log(den), lse_tile)
            o_ref[pr, out_rows, :] = jnp.where(lo, outs[0], outs[1]).astype(o_ref.dtype)
        lse_ref[out_rows, :] = lse_tile
        return carry

    lax.fori_loop(0, nblk, body, 0)


def _attention_group(qkv, bias, g, dilation, bsz, seq):
    r = dilation
    assert seq % TOKEN_TILE == 0 and TOKEN_TILE % (r * ATT_BLOCK) == 0
    ntile = seq // TOKEN_TILE
    per = TOKEN_TILE // r
    nblk = per // ATT_BLOCK
    gpw = QKV_WIDTH // WIDTH_A

    def cur(which):
        return lambda b, t, c: ((b * ntile + t) * r + c, which * gpw + g)

    def prev(which):
        return lambda b, t, c: (((b * ntile + jnp.maximum(t - 1, 0)) * r + c) * nblk + nblk - 1, which * gpw + g)

    return pl.pallas_call(
        functools.partial(_attn_kernel, nblk=nblk, r=r),
        grid=(bsz, ntile, r),
        in_specs=[
            pl.BlockSpec((per, WIDTH_A), cur(0)),
            pl.BlockSpec((per, WIDTH_A), cur(1)),
            pl.BlockSpec((per, WIDTH_A), cur(2)),
            pl.BlockSpec((ATT_BLOCK, WIDTH_A), prev(1)),
            pl.BlockSpec((ATT_BLOCK, WIDTH_A), prev(2)),
            _resident((HEADS_PER_GROUP, 2 * ATT_BLOCK), lambda b, t, c: (0, 0)),
        ],
        out_specs=[
            pl.BlockSpec((WIDTH_A // LANES, TOKEN_TILE, LANES), lambda b, t, c: (0, b * ntile + t, 0)),
            pl.BlockSpec((TOKEN_TILE, LANES), lambda b, t, c: (b * ntile + t, 0)),
        ],
        out_shape=[
            jax.ShapeDtypeStruct((WIDTH_A // LANES, bsz * seq, LANES), F32),
            jax.ShapeDtypeStruct((bsz * seq, LANES), F32),
        ],
        scratch_shapes=[
            pltpu.VMEM((per + ATT_BLOCK, WIDTH_A), BF16),
            pltpu.VMEM((per + ATT_BLOCK, WIDTH_A), BF16),
            pltpu.VMEM((2 * HEADS_PER_GROUP, ATT_BLOCK, 2 * ATT_BLOCK), F32),
        ],
        compiler_params=_params(("arbitrary", "arbitrary", "arbitrary")),
        name=f"attn_d{r}",
    )(qkv, qkv, qkv, qkv, qkv, bias)


def _t5_bucket(dist):
    max_exact = N_REL_BUCKETS // 2
    d = np.maximum(dist, 1).astype(np.float32)
    scale = (N_REL_BUCKETS - max_exact) / math.log(REL_MAX_DIST / max_exact)
    large = max_exact + (np.log(d / max_exact) * scale).astype(np.int32)
    large = np.minimum(large, N_REL_BUCKETS - 1)
    return np.where(dist < max_exact, dist, large).astype(np.int32)


def _group_rel_bias(rel_bias, g, dilation, n_steps):
    assert n_steps <= ATT_BLOCK
    dist = (ATT_BLOCK - np.arange(2 * ATT_BLOCK)) % (2 * ATT_BLOCK)
    cols = rel_bias[:, g * HEADS_PER_GROUP:(g + 1) * HEADS_PER_GROUP]
    vals = jnp.transpose(cols[_t5_bucket(dist * dilation)]).astype(F32)
    return jnp.where((dist <= n_steps)[None], vals, NEG_INF)


def _gmlp_kernel(z_ref, g_ref, b_ref, w_ref, bs_ref, o_ref, *, nchunk):
    z = jax.nn.gelu(z_ref[...])
    u = z[:, :WIDTH_B]
    v = _layer_norm_rows(z[:, WIDTH_B:], g_ref[...], b_ref[...]).astype(BF16)
    row = lax.broadcasted_iota(jnp.int32, (CHUNK, CHUNK), 0)
    col = lax.broadcasted_iota(jnp.int32, (CHUNK, CHUNK), 1)
    causal = row >= col
    bs = bs_ref[...]
    for g in range(N_GROUPS_B):
        cs = slice(g * LANES, (g + 1) * LANES)
        w = jnp.where(causal, w_ref[g], 0.0).astype(BF16)
        rhs = jnp.concatenate([v[n * CHUNK:(n + 1) * CHUNK, cs] for n in range(nchunk)], axis=1)
        mixed = jnp.dot(w, rhs, preferred_element_type=F32)
        for n in range(nchunk):
            rs = slice(n * CHUNK, (n + 1) * CHUNK)
            o_ref[rs, cs] = (u[rs, cs] * (mixed[:, n * LANES:(n + 1) * LANES] + bs[:, cs])).astype(o_ref.dtype)


def _spatial_gating(rest, zb_col, ln_g, ln_b, w_s, b_s, layer, tm):
    t = rest.shape[0]
    bs = jnp.repeat(jnp.transpose(b_s[layer]), LANES, axis=1)
    return pl.pallas_call(
        functools.partial(_gmlp_kernel, nchunk=tm // CHUNK),
        grid=(t // tm,),
        in_specs=[
            pl.BlockSpec((tm, 2 * WIDTH_B), lambda i: (i, zb_col // (2 * WIDTH_B))),
            pl.BlockSpec((None, 1, WIDTH_B), lambda i: (layer, 0, 0)),
            pl.BlockSpec((None, 1, WIDTH_B), lambda i: (layer, 0, 0)),
            pl.BlockSpec((None, N_GROUPS_B, CHUNK, CHUNK), lambda i: (layer, 0, 0, 0)),
            pl.BlockSpec((CHUNK, WIDTH_B), lambda i: (0, 0)),
        ],
        out_specs=pl.BlockSpec((tm, WIDTH_B), lambda i: (i, 0)),
        out_shape=jax.ShapeDtypeStruct((t, WIDTH_B), BF16),
        compiler_params=_params(("arbitrary",)),
        name="gmlp",
    )(rest, ln_g, ln_b, w_s, bs)


S5_CHUNK = 8
CHUNK_LANES = S5_CHUNK * SLAB
STATE_LANES = 2 * SLAB_STATES
N_POWERS = 2 * SUBLANES


def _s5_prep_kernel(lr_ref, li_ref, dt_ref, bre_ref, bim_ref, cre_ref, cim_ref, w_ref, mo_ref, coef_ref):
    nl, ns = S5_CHUNK, SLAB_STATES
    lr = lr_ref[...]
    li = li_ref[...]
    dt = jnp.exp(dt_ref[...])

    def apow(k):
        mag = jnp.exp(lr * dt * k)
        ang = li * dt * k
        return mag * jnp.cos(ang), mag * jnp.sin(ang)

    pr, pi = apow(lax.broadcasted_iota(jnp.int32, (N_POWERS, 1), 0).astype(F32))
    ab_re, ab_im = pr[1:2], pi[1:2]
    nrm = lr * lr + li * li
    zr = ((ab_re - 1.0) * lr + ab_im * li) / nrm
    zi = (ab_im * lr - (ab_re - 1.0) * li) / nrm
    dr = pr * zr - pi * zi
    di = pr * zi + pi * zr
    pr_c, pi_c, dr_c, di_c = pr.T, pi.T, dr.T, di.T
    bre, bim = bre_ref[...], bim_ref[...]
    cre, cim = cre_ref[...], cim_ref[...]
    mo_ref[0:CHUNK_LANES, :] = jnp.zeros((CHUNK_LANES, CHUNK_LANES), mo_ref.dtype)
    for e in range(nl):
        j = nl - 1 - e
        rows = slice(j * SLAB, (j + 1) * SLAB)
        w_ref[rows, 0:ns] = (bre * dr[e:e + 1] - bim * di[e:e + 1]).astype(w_ref.dtype)
        w_ref[rows, ns:2 * ns] = (bre * di[e:e + 1] + bim * dr[e:e + 1]).astype(w_ref.dtype)
        cdr = cre * dr_c[:, e:e + 1] - cim * di_c[:, e:e + 1]
        cdi = cre * di_c[:, e:e + 1] + cim * dr_c[:, e:e + 1]
        kt = (jnp.dot(bre, cdr, preferred_element_type=F32, precision=lax.Precision.HIGHEST)
              - jnp.dot(bim, cdi, preferred_element_type=F32, precision=lax.Precision.HIGHEST)).astype(mo_ref.dtype)
        for jj in range(nl - e):
            mo_ref[jj * SLAB:(jj + 1) * SLAB, (jj + e) * SLAB:(jj + e + 1) * SLAB] = kt
        cols = slice(e * SLAB, (e + 1) * SLAB)
        qr, qi = pr_c[:, e + 1:e + 2], pi_c[:, e + 1:e + 2]
        mo_ref[CHUNK_LANES:CHUNK_LANES + ns, cols] = (cre * qr - cim * qi).astype(mo_ref.dtype)
        mo_ref[CHUNK_LANES + ns:CHUNK_LANES + 2 * ns, cols] = (-(cre * qi + cim * qr)).astype(mo_ref.dtype)
    ar, ai = apow(((lax.broadcasted_iota(jnp.int32, (SUBLANES, 1), 0) + 1) * nl).astype(F32))
    rowi = lax.broadcasted_iota(jnp.int32, (SUBLANES, ns), 0)
    for idx, kk in enumerate((1, 2, 4)):
        coef_ref[2 * idx] = jnp.where(rowi >= kk, ar[kk - 1:kk], 0.0)
        coef_ref[2 * idx + 1] = jnp.where(rowi >= kk, ai[kk - 1:kk], 0.0)
    coef_ref[6] = ar
    coef_ref[7] = ai


def _s5_prep(lam_re, lam_im, log_dt, b_re, b_im, c_re, c_im):
    depth = lam_re.shape[0]
    g8, p, h = GROUPS_PER_SLAB, SSM_STATE, SSM_GROUP
    eye = jnp.eye(g8, dtype=F32)
    row = lambda a: a.astype(F32).reshape(depth, N_SLABS, 1, SLAB_STATES)
    dt = row(jnp.repeat(log_dt.astype(F32)[..., None], p, axis=-1))

    def b_diag(b):
        b = b.astype(F32).reshape(depth, N_SLABS, g8, p, h)
        return jnp.einsum("dsgph,gk->dsghkp", b, eye).reshape(depth, N_SLABS, SLAB, SLAB_STATES)

    def c_diag(c):
        c = c.astype(F32).reshape(depth, N_SLABS, g8, h, p)
        return jnp.einsum("dsghp,gk->dsgpkh", c, eye).reshape(depth, N_SLABS, SLAB_STATES, SLAB)

    blk = lambda a, b: pl.BlockSpec((None, None, a, b), lambda d, s: (d, s, 0, 0))
    return pl.pallas_call(
        _s5_prep_kernel,
        grid=(depth, N_SLABS),
        in_specs=[blk(1, SLAB_STATES)] * 3 + [blk(SLAB, SLAB_STATES)] * 2 + [blk(SLAB_STATES, SLAB)] * 2,
        out_specs=[blk(CHUNK_LANES, STATE_LANES), blk(CHUNK_LANES + STATE_LANES, CHUNK_LANES),
                   pl.BlockSpec((None, None, 8, SUBLANES, SLAB_STATES), lambda d, s: (d, s, 0, 0, 0))],
        out_shape=[jax.ShapeDtypeStruct((depth, N_SLABS, CHUNK_LANES, STATE_LANES), BF16),
                   jax.ShapeDtypeStruct((depth, N_SLABS, CHUNK_LANES + STATE_LANES, CHUNK_LANES), BF16),
                   jax.ShapeDtypeStruct((depth, N_SLABS, 8, SUBLANES, SLAB_STATES), F32)],
        compiler_params=_params(("arbitrary", "arbitrary")),
        name="s5_prep",
    )(row(lam_re), row(lam_im), dt, b_diag(b_re), b_diag(b_im), c_diag(c_re), c_diag(c_im))


def _s5_kernel(u_ref, w_ref, mo_ref, coef_ref, d_ref, o_ref, ubuf_ref):
    nl = S5_CHUNK
    seq = u_ref.shape[0]
    n = seq // nl
    nv = n // SUBLANES
    ns = SLAB_STATES
    ubuf_ref[0:nl, :] = jnp.zeros((nl, SLAB), F32)
    ubuf_ref[nl:, :] = u_ref[...]
    cur = [ubuf_ref[pl.ds(nl + j, n, stride=nl), :] for j in range(nl)]
    prev = [ubuf_ref[pl.ds(j, n, stride=nl), :] for j in range(nl)]
    u2 = jnp.concatenate([t.astype(BF16) for t in cur], axis=1)
    u2p = jnp.concatenate([t.astype(BF16) for t in prev], axis=1)
    sc = jnp.dot(u2p, w_ref[...], preferred_element_type=F32)
    xr = sc[:, :ns].reshape(nv, SUBLANES, ns)
    xi = sc[:, ns:].reshape(nv, SUBLANES, ns)
    for idx, k in enumerate((1, 2, 4)):
        cr = coef_ref[2 * idx]
        ci = coef_ref[2 * idx + 1]
        sr = pltpu.roll(xr, k, 1)
        si = pltpu.roll(xi, k, 1)
        xr, xi = xr + (cr * sr - ci * si), xi + (cr * si + ci * sr)
    pr = coef_ref[6]
    pi = coef_ref[7]
    c_r = jnp.zeros((1, ns), F32)
    c_i = jnp.zeros((1, ns), F32)
    out_r, out_i = [], []
    for r in range(nv):
        yr = xr[r] + (pr * c_r - pi * c_i)
        yi = xi[r] + (pr * c_i + pi * c_r)
        out_r.append(yr)
        out_i.append(yi)
        c_r = yr[SUBLANES - 1:SUBLANES, :]
        c_i = yi[SUBLANES - 1:SUBLANES, :]
    x = jnp.concatenate([jnp.concatenate(out_r, axis=0), jnp.concatenate(out_i, axis=0)], axis=1)
    lhs = jnp.concatenate([u2, x.astype(BF16)], axis=1)
    y = jnp.dot(lhs, mo_ref[...], preferred_element_type=F32)
    d = d_ref[...]
    for j in range(nl):
        o_ref[pl.ds(j, n, stride=nl), :] = jax.nn.gelu(y[:, j * SLAB:(j + 1) * SLAB] + d * cur[j]).astype(o_ref.dtype)


def _s5(zu, uc_col, wmat, mo, coef, d_skip, layer, bsz, seq):
    assert seq % (S5_CHUNK * SUBLANES) == 0
    zu3 = zu.reshape(bsz, seq, zu.shape[-1])
    c0 = uc_col // SLAB
    out = pl.pallas_call(
        _s5_kernel,
        grid=(N_SLABS, bsz),
        in_specs=[
            pl.BlockSpec((None, seq, SLAB), lambda s, b: (b, 0, c0 + s)),
            pl.BlockSpec((None, None, CHUNK_LANES, STATE_LANES), lambda s, b: (layer, s, 0, 0)),
            pl.BlockSpec((None, None, CHUNK_LANES + STATE_LANES, CHUNK_LANES), lambda s, b: (layer, s, 0, 0)),
            pl.BlockSpec((None, None, 8, SUBLANES, SLAB_STATES), lambda s, b: (layer, s, 0, 0, 0)),
            pl.BlockSpec((None, 1, SLAB), lambda s, b: (layer, 0, s)),
        ],
        out_specs=pl.BlockSpec((None, seq, SLAB), lambda s, b: (b, 0, s)),
        out_shape=jax.ShapeDtypeStruct((bsz, seq, WIDTH_C), F32),
        scratch_shapes=[pltpu.VMEM((seq + S5_CHUNK, SLAB), F32)],
        compiler_params=_params(("arbitrary", "arbitrary")),
        name="s5_scan",
    )(zu3, wmat, mo, coef, d_skip)
    return out.reshape(bsz * seq, WIDTH_C)


def _merge_kernel(o0_ref, o1_ref, o2_ref, l0_ref, l1_ref, l2_ref, yb_ref, yc_ref,
                  g0_ref, g1_ref, g2_ref, e_ref, wpa_ref, wpb_ref, wpc_ref, wglu_ref, bglu_ref, out_ref):
    l0, l1, l2 = l0_ref[...], l1_ref[...], l2_ref[...]
    m = jnp.maximum(jnp.maximum(l0, l1), l2)
    e0, e1, e2 = jnp.exp(l0 - m), jnp.exp(l1 - m), jnp.exp(l2 - m)
    den = e0 + e1 + e2
    e = e_ref[...]

    def expand(w):
        hi = w.astype(BF16)
        rem = w - hi.astype(F32)
        mid = rem.astype(BF16)
        lo = (rem - mid.astype(F32)).astype(BF16)
        return (jnp.dot(hi, e, preferred_element_type=F32) + jnp.dot(mid, e, preferred_element_type=F32)
                + jnp.dot(lo, e, preferred_element_type=F32))

    planes = lambda o_ref: jnp.concatenate([o_ref[p] for p in range(o_ref.shape[0])], axis=1)
    ya = (expand(e0 / den) * planes(o0_ref) + expand(e1 / den) * planes(o1_ref) + expand(e2 / den) * planes(o2_ref))
    pa = jnp.dot(ya.astype(BF16), wpa_ref[...], preferred_element_type=F32)
    pb = jnp.dot(yb_ref[...], wpb_ref[...], preferred_element_type=F32)
    yc = yc_ref[...]
    glu = yc * jax.nn.sigmoid(jnp.dot(yc.astype(BF16), wglu_ref[...], preferred_element_type=F32) + bglu_ref[...])
    pc = jnp.dot(glu.astype(BF16), wpc_ref[...], preferred_element_type=F32)
    merged = (g0_ref[...].astype(F32) * pa + g1_ref[...].astype(F32) * pb + g2_ref[...].astype(F32) * pc)
    out_ref[...] = merged.astype(out_ref.dtype)


def _resident(shape, index_map):
    return pl.BlockSpec(shape, index_map, pipeline_mode=pl.Buffered(1))


def _merge(outs, lses, yb, yc, gates, expand, w_pa, w_pb, w_pc, w_glu, b_glu, layer, d_model, tm):
    t = yb.shape[0]
    row = lambda w: pl.BlockSpec((tm, w), lambda i: (i, 0))
    full = lambda a, b: _resident((None, a, b), lambda i: (layer, 0, 0))
    return pl.pallas_call(
        _merge_kernel,
        grid=(t // tm,),
        in_specs=[pl.BlockSpec((WIDTH_A // LANES, tm, LANES), lambda i: (0, i, 0))] * 3
        + [row(LANES)] * 3 + [row(WIDTH_B), row(WIDTH_C)]
        + [pl.BlockSpec((tm, d_model), lambda i, k=k: (i, k)) for k in range(N_BRANCH)]
        + [_resident((LANES, WIDTH_A), lambda i: (0, 0)),
           full(WIDTH_A, d_model), full(WIDTH_B, d_model), full(WIDTH_C, d_model),
           full(WIDTH_C, WIDTH_C), full(1, WIDTH_C)],
        out_specs=pl.BlockSpec((tm, d_model), lambda i: (i, 0)),
        out_shape=jax.ShapeDtypeStruct((t, d_model), BF16),
        compiler_params=_params(("arbitrary",)),
        name="merge",
    )(*outs, *lses, yb, yc, gates, gates, gates, expand, w_pa, w_pb, w_pc, w_glu, b_glu)


def _proj_ln_kernel(m_ref, x_ref, w_ref, g_ref, b_ref, o_ref, ob_ref, *, alpha):
    f = jnp.dot(m_ref[...], w_ref[...], preferred_element_type=F32)
    y = _layer_norm_rows(alpha * x_ref[...] + f, g_ref[...], b_ref[...])
    o_ref[...] = y
    ob_ref[...] = y.astype(BF16)


def _proj_ln(merged, x, w_o, ln_g, ln_b, layer, alpha, tm):
    t, d = x.shape
    return pl.pallas_call(
        functools.partial(_proj_ln_kernel, alpha=alpha),
        grid=(t // tm,),
        in_specs=[
            pl.BlockSpec((tm, d), lambda i: (i, 0)),
            pl.BlockSpec((tm, d), lambda i: (i, 0)),
            _resident((None, d, d), lambda i: (layer, 0, 0)),
            _resident((None, 1, d), lambda i: (layer, 0, 0)),
            _resident((None, 1, d), lambda i: (layer, 0, 0)),
        ],
        out_specs=[pl.BlockSpec((tm, d), lambda i: (i, 0)), pl.BlockSpec((tm, d), lambda i: (i, 0))],
        out_shape=[jax.ShapeDtypeStruct((t, d), F32), jax.ShapeDtypeStruct((t, d), BF16)],
        compiler_params=_params(("arbitrary",)),
        name="out_proj_ln",
    )(merged, x, w_o, ln_g, ln_b)


def _swiglu_kernel(x_ref, wg_ref, wu_ref, o_ref, wgb_ref, wub_ref):
    @pl.when(pl.program_id(1) == 0)
    def _():
        wgb_ref[...] = wg_ref[...].astype(BF16)
        wub_ref[...] = wu_ref[...].astype(BF16)

    x = x_ref[...]
    gate = jnp.dot(x, wgb_ref[...], preferred_element_type=F32)
    up = jnp.dot(x, wub_ref[...], preferred_element_type=F32)
    o_ref[...] = (jax.nn.silu(gate) * up).astype(o_ref.dtype)


def _swiglu(xb, w_in, layer, d_ff, bm, bn):
    t, d = xb.shape
    nj = d_ff // bn
    return pl.pallas_call(
        _swiglu_kernel,
        grid=(nj, t // bm),
        in_specs=[
            pl.BlockSpec((bm, d), lambda j, i: (i, 0)),
            pl.BlockSpec((None, d, bn), lambda j, i: (layer, 0, j)),
            pl.BlockSpec((None, d, bn), lambda j, i: (layer, 0, j + nj)),
        ],
        out_specs=pl.BlockSpec((bm, bn), lambda j, i: (i, j)),
        out_shape=jax.ShapeDtypeStruct((t, d_ff), BF16),
        scratch_shapes=[pltpu.VMEM((d, bn), BF16), pltpu.VMEM((d, bn), BF16)],
        compiler_params=_params(("arbitrary", "arbitrary")),
        name="ffn_in",
    )(xb, w_in, w_in)


def _ffn_out_ln_kernel(h_ref, w_ref, x_ref, g_ref, b_ref, o_ref, ob_ref, acc_ref, *, alpha, nk):
    k = pl.program_id(1)
    part = jnp.dot(h_ref[...], w_ref[...], preferred_element_type=F32)

    @pl.when(k == 0)
    def _():
        acc_ref[...] = part

    @pl.when(k > 0)
    def _():
        acc_ref[...] += part

    @pl.when(k == nk - 1)
    def _():
        y = _layer_norm_rows(alpha * x_ref[...] + acc_ref[...], g_ref[...], b_ref[...])
        o_ref[...] = y
        ob_ref[...] = y.astype(BF16)


def _ffn_out_ln(h, x, w_out, ln_g, ln_b, layer, alpha, bm, bk):
    t, d = x.shape
    d_ff = h.shape[1]
    nk = d_ff // bk
    return pl.pallas_call(
        functools.partial(_ffn_out_ln_kernel, alpha=alpha, nk=nk),
        grid=(t // bm, nk),
        in_specs=[
            pl.BlockSpec((bm, bk), lambda i, k: (i, k)),
            pl.BlockSpec((None, bk, d), lambda i, k: (layer, k, 0)),
            pl.BlockSpec((bm, d), lambda i, k: (i, 0)),
            pl.BlockSpec((None, 1, d), lambda i, k: (layer, 0, 0)),
            pl.BlockSpec((None, 1, d), lambda i, k: (layer, 0, 0)),
        ],
        out_specs=[pl.BlockSpec((bm, d), lambda i, k: (i, 0)), pl.BlockSpec((bm, d), lambda i, k: (i, 0))],
        out_shape=[jax.ShapeDtypeStruct((t, d), F32), jax.ShapeDtypeStruct((t, d), BF16)],
        scratch_shapes=[pltpu.VMEM((bm, d), F32)],
        compiler_params=_params(("arbitrary", "arbitrary")),
        name="ffn_out_ln",
    )(h, w_out, x, ln_g, ln_b)


@jax.jit
def kernel(x, w_in, b_in, rel_bias, sgu_ln_g, sgu_ln_b, w_s, b_s, lam_re, lam_im, log_dt, b_re, b_im, c_re, c_im, d_skip, w_glu, b_glu, w_pa, w_pb, w_pc, w_o, ln1_g, ln1_b, w_ffn_in, w_ffn_out, ln2_g, ln2_b):
    bsz, seq, d_model = x.shape
    depth = w_in.shape[0]
    d_ff = w_ffn_out.shape[1]
    t = bsz * seq
    alpha = (2 * depth) ** 0.25
    in_cols = w_in.shape[-1]
    gl_width = N_BRANCH * d_model
    zb_src = 3 * QKV_WIDTH
    uc_src = zb_src + 2 * WIDTH_B
    gl_src = uc_src + WIDTH_C
    assert gl_src + gl_width == in_cols
    bn = 768
    zu_width = 2 * WIDTH_B + WIDTH_C
    zb_col, uc_col = 0, 2 * WIDTH_B

    bf = lambda a: a.astype(BF16)
    row3 = lambda a: a.reshape(a.shape[0], 1, a.shape[1]).astype(F32)
    w_pa_b, w_pb_b, w_pc_b, w_glu_b = bf(w_pa), bf(w_pb), bf(w_pc), bf(w_glu)
    w_o_b, w_ffn_out_b = bf(w_o), bf(w_ffn_out)
    w_in32, w_ffn_in32 = w_in.astype(F32), w_ffn_in.astype(F32)
    b_in3, b_glu3, d_skip3 = row3(b_in), row3(b_glu), row3(d_skip)
    sgu_g3, sgu_b3 = row3(sgu_ln_g), row3(sgu_ln_b)
    ln1_g3, ln1_b3, ln2_g3, ln2_b3 = row3(ln1_g), row3(ln1_b), row3(ln2_g), row3(ln2_b)
    w_s32 = w_s.astype(F32)

    group_bias = [_group_rel_bias(rel_bias, g, dil, win // dil) for g, (win, dil) in enumerate(ATT_PATTERNS)]
    expand = jnp.asarray(np.kron(np.eye(LANES, HEADS_PER_GROUP, dtype=np.float32),
                                 np.ones((1, HEAD_DIM), np.float32)), dtype=BF16)

    s5_w, s5_mo, s5_coef = _s5_prep(lam_re, lam_im, log_dt, b_re, b_im, c_re, c_im)

    xf = x.reshape(t, d_model).astype(F32)
    xb = bf(xf)
    for l in range(depth):
        qkv = _in_proj(xb, w_in32, b_in3, l, lambda j: j, 3 * QKV_WIDTH, BF16, TOKEN_TILE, WIDTH_A, "streams")
        zu = _in_proj(xb, w_in32, b_in3, l, lambda j: j + zb_src // bn, zu_width, F32, TOKEN_TILE // 2, bn, "plain")
        gates = _in_proj(xb, w_in32, b_in3, l, lambda j: j + gl_src // bn, gl_width, BF16, TOKEN_TILE, bn,
                         "sigmoid")
        outs, lses = [], []
        for g, (window, dil) in enumerate(ATT_PATTERNS):
            o_g, lse_g = _attention_group(qkv, group_bias[g], g, dil, bsz, seq)
            outs.append(o_g)
            lses.append(lse_g)
        yb = _spatial_gating(zu, zb_col, sgu_g3, sgu_b3, w_s32, b_s, l, 512)
        yc = _s5(zu, uc_col, s5_w, s5_mo, s5_coef, d_skip3, l, bsz, seq)
        merged = _merge(outs, lses, yb, yc, gates, expand, w_pa_b, w_pb_b, w_pc_b, w_glu_b, b_glu3,
                        l, d_model, 512)
        xf, xb = _proj_ln(merged, xf, w_o_b, ln1_g3, ln1_b3, l, alpha, 512)
        h = _swiglu(xb, w_ffn_in32, l, d_ff, 1024, 512)
        xf, xb = _ffn_out_ln(h, xf, w_ffn_out_b, ln2_g3, ln2_b3, l, alpha, 512, 1408)
    return xf.reshape(bsz, seq, d_model).astype(x.dtype)
```

```python
import functools
import math

import numpy as np
import jax
import jax.numpy as jnp
from jax import lax
from jax.experimental import pallas as pl
from jax.experimental.pallas import tpu as pltpu

F32 = jnp.float32
BF16 = jnp.bfloat16

ATT_PATTERNS = ((128, 1), (512, 4), (2048, 16))
HEADS_PER_GROUP = 8
HEAD_DIM = 64
N_HEADS = len(ATT_PATTERNS) * HEADS_PER_GROUP
QKV_WIDTH = N_HEADS * HEAD_DIM
WIDTH_A = HEADS_PER_GROUP * HEAD_DIM
ATT_BLOCK = 128
N_REL_BUCKETS = 32
REL_MAX_DIST = 2048
NEG_INF = -1e30
CHUNK = 128
WIDTH_B = 768
N_GROUPS_B = 6
WIDTH_C = 768
SSM_GROUP = 16
N_GROUPS_C = WIDTH_C // SSM_GROUP
SSM_STATE = 64
N_BRANCH = 3
LN_EPS = 1e-5

LANES = 128
SUBLANES = 8
VMEM_LIMIT = 48 * 1024 * 1024
VMEM_LIMIT_LARGE = 56 * 1024 * 1024

SLAB = LANES
GROUPS_PER_SLAB = SLAB // SSM_GROUP
N_SLABS = WIDTH_C // SLAB
SLAB_STATES = GROUPS_PER_SLAB * SSM_STATE


def _params(sem, vmem=VMEM_LIMIT):
    return pltpu.CompilerParams(dimension_semantics=sem, vmem_limit_bytes=vmem)


def _layer_norm_rows(y, g, b):
    mu = jnp.mean(y, axis=-1, keepdims=True)
    yc = y - mu
    var = jnp.mean(yc * yc, axis=-1, keepdims=True)
    return yc * lax.rsqrt(var + LN_EPS) * g + b


TOKEN_TILE = 2048
ATTN_UNROLL = 4


def _in_proj_kernel(x_ref, w_ref, b_ref, o_ref, wb_ref, *scratch, mode):
    @pl.when(pl.program_id(1) == 0)
    def _():
        wb_ref[...] = w_ref[...].astype(BF16)

    acc = jnp.dot(x_ref[...], wb_ref[...], preferred_element_type=F32) + b_ref[...]
    if mode == "plain":
        o_ref[...] = acc.astype(o_ref.dtype)
    elif mode == "sigmoid":
        o_ref[...] = (0.5 * jnp.tanh(0.5 * acc) + 0.5).astype(o_ref.dtype)
    else:
        (scr_ref,) = scratch
        grp = pl.program_id(0) % len(ATT_PATTERNS)
        rows = o_ref.shape[0]
        for gi, (_, r) in enumerate(ATT_PATTERNS):
            @pl.when(grp == gi)
            def _(r=r):
                if r == 1:
                    o_ref[...] = acc.astype(o_ref.dtype)
                else:
                    per = rows // r
                    for kt in range(o_ref.shape[1] // LANES):
                        cs = slice(kt * LANES, (kt + 1) * LANES)
                        scr_ref[kt] = acc[:, cs]
                        for c in range(r):
                            o_ref[c * per:(c + 1) * per, cs] = (
                                scr_ref[kt, pl.ds(c, per, stride=r), :].astype(o_ref.dtype))


def _in_proj(x, w, b, layer, col_map, n_out, out_dtype, bm, bn, mode):
    m, k = x.shape
    scratch = [pltpu.VMEM((k, bn), BF16)]
    if mode == "streams":
        scratch.append(pltpu.VMEM((bn // LANES, bm, LANES), F32))
    return pl.pallas_call(
        functools.partial(_in_proj_kernel, mode=mode),
        grid=(n_out // bn, m // bm),
        in_specs=[
            pl.BlockSpec((bm, k), lambda j, i: (i, 0)),
            pl.BlockSpec((None, k, bn), lambda j, i: (layer, 0, col_map(j))),
            pl.BlockSpec((None, 1, bn), lambda j, i: (layer, 0, col_map(j))),
        ],
        out_specs=pl.BlockSpec((bm, bn), lambda j, i: (i, j)),
        out_shape=jax.ShapeDtypeStruct((m, n_out), out_dtype),
        scratch_shapes=scratch,
        compiler_params=_params(("arbitrary", "arbitrary"), VMEM_LIMIT_LARGE),
        name="in_proj_" + mode,
    )(x, w, b)


def _attn_kernel(q_ref, k_ref, v_ref, kp_ref, vp_ref, base_ref, o_ref, lse_ref, kf_ref, vf_ref, bias_ref,
                 *, nblk, r, sg):
    q_blk = ATT_BLOCK
    per = nblk * q_blk
    tile = pl.program_id(1)
    cg = pl.program_id(2)

    @pl.when(jnp.logical_and(pl.program_id(0) == 0, jnp.logical_and(tile == 0, cg == 0)))
    def _():
        col = lax.broadcasted_iota(jnp.int32, (q_blk, 2 * q_blk), 1)
        for h in range(HEADS_PER_GROUP):
            base = jnp.broadcast_to(base_ref[h:h + 1, :], (q_blk, 2 * q_blk))
            tab_h = pltpu.roll(base, 0, 1, stride=1, stride_axis=0)
            bias_ref[h] = tab_h
            bias_ref[HEADS_PER_GROUP + h] = jnp.where(col >= q_blk, tab_h, NEG_INF)

    for st in range(sg):
        dst = st * (per + q_blk)
        kf_ref[dst:dst + q_blk, :] = kp_ref[st * q_blk:(st + 1) * q_blk, :]
        kf_ref[dst + q_blk:dst + q_blk + per, :] = k_ref[st * per:(st + 1) * per, :]
        vf_ref[dst:dst + q_blk, :] = vp_ref[st * q_blk:(st + 1) * q_blk, :]
        vf_ref[dst + q_blk:dst + q_blk + per, :] = v_ref[st * per:(st + 1) * per, :]
    lane = lax.broadcasted_iota(jnp.int32, (q_blk, LANES), 1)
    lo = lane < HEAD_DIM
    scale = HEAD_DIM ** -0.5

    def body(u, carry):
        st, n = (0, u) if sg == 1 else (u, 0)
        r0 = pl.multiple_of(u * q_blk, q_blk)
        k0 = pl.multiple_of((u + st) * q_blk, q_blk)
        out_rows = pl.ds(r0, q_blk) if r == 1 else pl.ds(n * (q_blk * r) + cg * sg + st, q_blk, stride=r)
        tab = jnp.where(jnp.logical_and(tile == 0, n == 0), HEADS_PER_GROUP, 0)
        lse_tile = jnp.zeros((q_blk, LANES), F32)
        for pr in range(HEADS_PER_GROUP // 2):
            cs = slice(pr * LANES, (pr + 1) * LANES)
            q2 = q_ref[pl.ds(r0, q_blk), cs]
            k2 = kf_ref[pl.ds(k0, 2 * q_blk), cs]
            v2 = vf_ref[pl.ds(k0, 2 * q_blk), cs]
            outs = []
            for hh in range(2):
                h = 2 * pr + hh
                sel = lo if hh == 0 else jnp.logical_not(lo)
                qm = jnp.where(sel, q2 * scale, jnp.zeros_like(q2))
                s = lax.dot_general(qm, k2, (((1,), (1,)), ((), ())), preferred_element_type=F32)
                s = s + bias_ref[tab + h]
                m = jnp.max(s, axis=-1, keepdims=True)
                p = jnp.exp(s - m)
                den = jnp.sum(p, axis=-1, keepdims=True)
                o = jnp.dot(p.astype(BF16), v2, preferred_element_type=F32) / den
                outs.append(o)
                lse_tile = jnp.where(lane == h, m + jnp.log(den), lse_tile)
            o_ref[pr, out_rows, :] = jnp.where(lo, outs[0], outs[1]).astype(o_ref.dtype)
        lse_ref[out_rows, :] = lse_tile
        return carry

    lax.fori_loop(0, sg * nblk, body, 0, unroll=min(sg * nblk, ATTN_UNROLL))


def _attention_group(qkv, bias, g, dilation, bsz, seq):
    r = dilation
    assert seq % TOKEN_TILE == 0 and TOKEN_TILE % (r * ATT_BLOCK) == 0
    ntile = seq // TOKEN_TILE
    per = TOKEN_TILE // r
    nblk = per // ATT_BLOCK
    sg = min(r, ATTN_UNROLL) if nblk == 1 else 1
    ncg = r // sg
    gpw = QKV_WIDTH // WIDTH_A

    def cur(which):
        return lambda b, t, c: ((b * ntile + t) * ncg + c, which * gpw + g)

    def prev(which):
        return lambda b, t, c: (((b * ntile + jnp.maximum(t - 1, 0)) * ncg + c) * nblk + nblk - 1, which * gpw + g)

    return pl.pallas_call(
        functools.partial(_attn_kernel, nblk=nblk, r=r, sg=sg),
        grid=(bsz, ntile, ncg),
        in_specs=[
            pl.BlockSpec((sg * per, WIDTH_A), cur(0)),
            pl.BlockSpec((sg * per, WIDTH_A), cur(1)),
            pl.BlockSpec((sg * per, WIDTH_A), cur(2)),
            pl.BlockSpec((sg * ATT_BLOCK, WIDTH_A), prev(1)),
            pl.BlockSpec((sg * ATT_BLOCK, WIDTH_A), prev(2)),
            _resident((HEADS_PER_GROUP, 2 * ATT_BLOCK), lambda b, t, c: (0, 0)),
        ],
        out_specs=[
            pl.BlockSpec((WIDTH_A // LANES, TOKEN_TILE, LANES), lambda b, t, c: (0, b * ntile + t, 0)),
            pl.BlockSpec((TOKEN_TILE, LANES), lambda b, t, c: (b * ntile + t, 0)),
        ],
        out_shape=[
            jax.ShapeDtypeStruct((WIDTH_A // LANES, bsz * seq, LANES), F32),
            jax.ShapeDtypeStruct((bsz * seq, LANES), F32),
        ],
        scratch_shapes=[
            pltpu.VMEM((sg * (per + ATT_BLOCK), WIDTH_A), BF16),
            pltpu.VMEM((sg * (per + ATT_BLOCK), WIDTH_A), BF16),
            pltpu.VMEM((2 * HEADS_PER_GROUP, ATT_BLOCK, 2 * ATT_BLOCK), F32),
        ],
        compiler_params=_params(("arbitrary", "arbitrary", "arbitrary")),
        name=f"attn_d{r}",
    )(qkv, qkv, qkv, qkv, qkv, bias)


def _t5_bucket(dist):
    max_exact = N_REL_BUCKETS // 2
    d = np.maximum(dist, 1).astype(np.float32)
    scale = (N_REL_BUCKETS - max_exact) / math.log(REL_MAX_DIST / max_exact)
    large = max_exact + (np.log(d / max_exact) * scale).astype(np.int32)
    large = np.minimum(large, N_REL_BUCKETS - 1)
    return np.where(dist < max_exact, dist, large).astype(np.int32)


def _group_rel_bias(rel_bias, g, dilation, n_steps):
    assert n_steps <= ATT_BLOCK
    dist = (ATT_BLOCK - np.arange(2 * ATT_BLOCK)) % (2 * ATT_BLOCK)
    cols = rel_bias[:, g * HEADS_PER_GROUP:(g + 1) * HEADS_PER_GROUP]
    vals = jnp.transpose(cols[_t5_bucket(dist * dilation)]).astype(F32)
    return jnp.where((dist <= n_steps)[None], vals, NEG_INF)


def _gmlp_kernel(z_ref, g_ref, b_ref, w_ref, bs_ref, o_ref, *, nchunk):
    z = jax.nn.gelu(z_ref[...])
    u = z[:, :WIDTH_B]
    v = _layer_norm_rows(z[:, WIDTH_B:], g_ref[...], b_ref[...]).astype(BF16)
    row = lax.broadcasted_iota(jnp.int32, (CHUNK, CHUNK), 0)
    col = lax.broadcasted_iota(jnp.int32, (CHUNK, CHUNK), 1)
    causal = row >= col
    bs = bs_ref[...]
    for g in range(N_GROUPS_B):
        cs = slice(g * LANES, (g + 1) * LANES)
        w = jnp.where(causal, w_ref[g], 0.0).astype(BF16)
        rhs = jnp.concatenate([v[n * CHUNK:(n + 1) * CHUNK, cs] for n in range(nchunk)], axis=1)
        mixed = jnp.dot(w, rhs, preferred_element_type=F32)
        for n in range(nchunk):
            rs = slice(n * CHUNK, (n + 1) * CHUNK)
            o_ref[rs, cs] = (u[rs, cs] * (mixed[:, n * LANES:(n + 1) * LANES] + bs[:, cs])).astype(o_ref.dtype)


def _spatial_gating(rest, zb_col, ln_g, ln_b, w_s, b_s, layer, tm):
    t = rest.shape[0]
    bs = jnp.repeat(jnp.transpose(b_s[layer]), LANES, axis=1)
    return pl.pallas_call(
        functools.partial(_gmlp_kernel, nchunk=tm // CHUNK),
        grid=(t // tm,),
        in_specs=[
            pl.BlockSpec((tm, 2 * WIDTH_B), lambda i: (i, zb_col // (2 * WIDTH_B))),
            pl.BlockSpec((None, 1, WIDTH_B), lambda i: (layer, 0, 0)),
            pl.BlockSpec((None, 1, WIDTH_B), lambda i: (layer, 0, 0)),
            pl.BlockSpec((None, N_GROUPS_B, CHUNK, CHUNK), lambda i: (layer, 0, 0, 0)),
            pl.BlockSpec((CHUNK, WIDTH_B), lambda i: (0, 0)),
        ],
        out_specs=pl.BlockSpec((tm, WIDTH_B), lambda i: (i, 0)),
        out_shape=jax.ShapeDtypeStruct((t, WIDTH_B), BF16),
        compiler_params=_params(("arbitrary",)),
        name="gmlp",
    )(rest, ln_g, ln_b, w_s, bs)


S5_CHUNK = 8
CHUNK_LANES = S5_CHUNK * SLAB
STATE_LANES = 2 * SLAB_STATES
N_POWERS = 2 * SUBLANES


def _s5_prep_kernel(lr_ref, li_ref, dt_ref, bre_ref, bim_ref, cre_ref, cim_ref, w_ref, mo_ref, coef_ref):
    nl, ns = S5_CHUNK, SLAB_STATES
    lr = lr_ref[...]
    li = li_ref[...]
    dt = jnp.exp(dt_ref[...])

    def apow(k):
        mag = jnp.exp(lr * dt * k)
        ang = li * dt * k
        return mag * jnp.cos(ang), mag * jnp.sin(ang)

    pr, pi = apow(lax.broadcasted_iota(jnp.int32, (N_POWERS, 1), 0).astype(F32))
    ab_re, ab_im = pr[1:2], pi[1:2]
    nrm = lr * lr + li * li
    zr = ((ab_re - 1.0) * lr + ab_im * li) / nrm
    zi = (ab_im * lr - (ab_re - 1.0) * li) / nrm
    dr = pr * zr - pi * zi
    di = pr * zi + pi * zr
    pr_c, pi_c, dr_c, di_c = pr.T, pi.T, dr.T, di.T
    bre, bim = bre_ref[...], bim_ref[...]
    cre, cim = cre_ref[...], cim_ref[...]
    mo_ref[0:CHUNK_LANES, :] = jnp.zeros((CHUNK_LANES, CHUNK_LANES), mo_ref.dtype)
    for e in range(nl):
        j = nl - 1 - e
        rows = slice(j * SLAB, (j + 1) * SLAB)
        w_ref[rows, 0:ns] = (bre * dr[e:e + 1] - bim * di[e:e + 1]).astype(w_ref.dtype)
        w_ref[rows, ns:2 * ns] = (bre * di[e:e + 1] + bim * dr[e:e + 1]).astype(w_ref.dtype)
        cdr = cre * dr_c[:, e:e + 1] - cim * di_c[:, e:e + 1]
        cdi = cre * di_c[:, e:e + 1] + cim * dr_c[:, e:e + 1]
        kt = (jnp.dot(bre, cdr, preferred_element_type=F32, precision=lax.Precision.HIGHEST)
              - jnp.dot(bim, cdi, preferred_element_type=F32, precision=lax.Precision.HIGHEST)).astype(mo_ref.dtype)
        for jj in range(nl - e):
            mo_ref[jj * SLAB:(jj + 1) * SLAB, (jj + e) * SLAB:(jj + e + 1) * SLAB] = kt
        cols = slice(e * SLAB, (e + 1) * SLAB)
        qr, qi = pr_c[:, e + 1:e + 2], pi_c[:, e + 1:e + 2]
        mo_ref[CHUNK_LANES:CHUNK_LANES + ns, cols] = (cre * qr - cim * qi).astype(mo_ref.dtype)
        mo_ref[CHUNK_LANES + ns:CHUNK_LANES + 2 * ns, cols] = (-(cre * qi + cim * qr)).astype(mo_ref.dtype)
    ar, ai = apow(((lax.broadcasted_iota(jnp.int32, (SUBLANES, 1), 0) + 1) * nl).astype(F32))
    rowi = lax.broadcasted_iota(jnp.int32, (SUBLANES, ns), 0)
    for idx, kk in enumerate((1, 2, 4)):
        coef_ref[2 * idx] = jnp.where(rowi >= kk, ar[kk - 1:kk], 0.0)
        coef_ref[2 * idx + 1] = jnp.where(rowi >= kk, ai[kk - 1:kk], 0.0)
    coef_ref[6] = ar
    coef_ref[7] = ai


def _s5_prep(lam_re, lam_im, log_dt, b_re, b_im, c_re, c_im):
    depth = lam_re.shape[0]
    g8, p, h = GROUPS_PER_SLAB, SSM_STATE, SSM_GROUP
    eye = jnp.eye(g8, dtype=F32)
    row = lambda a: a.astype(F32).reshape(depth, N_SLABS, 1, SLAB_STATES)
    dt = row(jnp.repeat(log_dt.astype(F32)[..., None], p, axis=-1))

    def b_diag(b):
        b = b.astype(F32).reshape(depth, N_SLABS, g8, p, h)
        return jnp.einsum("dsgph,gk->dsghkp", b, eye).reshape(depth, N_SLABS, SLAB, SLAB_STATES)

    def c_diag(c):
        c = c.astype(F32).reshape(depth, N_SLABS, g8, h, p)
        return jnp.einsum("dsghp,gk->dsgpkh", c, eye).reshape(depth, N_SLABS, SLAB_STATES, SLAB)

    blk = lambda a, b: pl.BlockSpec((None, None, a, b), lambda d, s: (d, s, 0, 0))
    return pl.pallas_call(
        _s5_prep_kernel,
        grid=(depth, N_SLABS),
        in_specs=[blk(1, SLAB_STATES)] * 3 + [blk(SLAB, SLAB_STATES)] * 2 + [blk(SLAB_STATES, SLAB)] * 2,
        out_specs=[blk(CHUNK_LANES, STATE_LANES), blk(CHUNK_LANES + STATE_LANES, CHUNK_LANES),
                   pl.BlockSpec((None, None, 8, SUBLANES, SLAB_STATES), lambda d, s: (d, s, 0, 0, 0))],
        out_shape=[jax.ShapeDtypeStruct((depth, N_SLABS, CHUNK_LANES, STATE_LANES), BF16),
                   jax.ShapeDtypeStruct((depth, N_SLABS, CHUNK_LANES + STATE_LANES, CHUNK_LANES), BF16),
                   jax.ShapeDtypeStruct((depth, N_SLABS, 8, SUBLANES, SLAB_STATES), F32)],
        compiler_params=_params(("arbitrary", "arbitrary")),
        name="s5_prep",
    )(row(lam_re), row(lam_im), dt, b_diag(b_re), b_diag(b_im), c_diag(c_re), c_diag(c_im))


def _s5_kernel(u_ref, w_ref, mo_ref, coef_ref, d_ref, o_ref, ubuf_ref):
    nl = S5_CHUNK
    seq = u_ref.shape[0]
    n = seq // nl
    nv = n // SUBLANES
    ns = SLAB_STATES
    ubuf_ref[0:nl, :] = jnp.zeros((nl, SLAB), F32)
    ubuf_ref[nl:, :] = u_ref[...]
    cur = [ubuf_ref[pl.ds(nl + j, n, stride=nl), :] for j in range(nl)]
    prev = [ubuf_ref[pl.ds(j, n, stride=nl), :] for j in range(nl)]
    u2 = jnp.concatenate([t.astype(BF16) for t in cur], axis=1)
    u2p = jnp.concatenate([t.astype(BF16) for t in prev], axis=1)
    sc = jnp.dot(u2p, w_ref[...], preferred_element_type=F32)
    xr = sc[:, :ns].reshape(nv, SUBLANES, ns)
    xi = sc[:, ns:].reshape(nv, SUBLANES, ns)
    for idx, k in enumerate((1, 2, 4)):
        cr = coef_ref[2 * idx]
        ci = coef_ref[2 * idx + 1]
        sr = pltpu.roll(xr, k, 1)
        si = pltpu.roll(xi, k, 1)
        xr, xi = xr + (cr * sr - ci * si), xi + (cr * si + ci * sr)
    pr = coef_ref[6]
    pi = coef_ref[7]
    c_r = jnp.zeros((1, ns), F32)
    c_i = jnp.zeros((1, ns), F32)
    out_r, out_i = [], []
    for r in range(nv):
        yr = xr[r] + (pr * c_r - pi * c_i)
        yi = xi[r] + (pr * c_i + pi * c_r)
        out_r.append(yr)
        out_i.append(yi)
        c_r = yr[SUBLANES - 1:SUBLANES, :]
        c_i = yi[SUBLANES - 1:SUBLANES, :]
    x = jnp.concatenate([jnp.concatenate(out_r, axis=0), jnp.concatenate(out_i, axis=0)], axis=1)
    lhs = jnp.concatenate([u2, x.astype(BF16)], axis=1)
    y = jnp.dot(lhs, mo_ref[...], preferred_element_type=F32)
    d = d_ref[...]
    for j in range(nl):
        o_ref[pl.ds(j, n, stride=nl), :] = jax.nn.gelu(y[:, j * SLAB:(j + 1) * SLAB] + d * cur[j]).astype(o_ref.dtype)


def _s5(zu, uc_col, wmat, mo, coef, d_skip, layer, bsz, seq):
    assert seq % (S5_CHUNK * SUBLANES) == 0
    zu3 = zu.reshape(bsz, seq, zu.shape[-1])
    c0 = uc_col // SLAB
    out = pl.pallas_call(
        _s5_kernel,
        grid=(N_SLABS, bsz),
        in_specs=[
            pl.BlockSpec((None, seq, SLAB), lambda s, b: (b, 0, c0 + s)),
            pl.BlockSpec((None, None, CHUNK_LANES, STATE_LANES), lambda s, b: (layer, s, 0, 0)),
            pl.BlockSpec((None, None, CHUNK_LANES + STATE_LANES, CHUNK_LANES), lambda s, b: (layer, s, 0, 0)),
            pl.BlockSpec((None, None, 8, SUBLANES, SLAB_STATES), lambda s, b: (layer, s, 0, 0, 0)),
            pl.BlockSpec((None, 1, SLAB), lambda s, b: (layer, 0, s)),
        ],
        out_specs=pl.BlockSpec((None, seq, SLAB), lambda s, b: (b, 0, s)),
        out_shape=jax.ShapeDtypeStruct((bsz, seq, WIDTH_C), F32),
        scratch_shapes=[pltpu.VMEM((seq + S5_CHUNK, SLAB), F32)],
        compiler_params=_params(("arbitrary", "arbitrary")),
        name="s5_scan",
    )(zu3, wmat, mo, coef, d_skip)
    return out.reshape(bsz * seq, WIDTH_C)


def _merge_kernel(o0_ref, o1_ref, o2_ref, l0_ref, l1_ref, l2_ref, yb_ref, yc_ref,
                  g0_ref, g1_ref, g2_ref, e_ref, wpa_ref, wpb_ref, wpc_ref, wglu_ref, bglu_ref, out_ref):
    l0, l1, l2 = l0_ref[...], l1_ref[...], l2_ref[...]
    m = jnp.maximum(jnp.maximum(l0, l1), l2)
    e0, e1, e2 = jnp.exp(l0 - m), jnp.exp(l1 - m), jnp.exp(l2 - m)
    den = e0 + e1 + e2
    e = e_ref[...]

    def expand(w):
        hi = w.astype(BF16)
        mid = (w - hi.astype(F32)).astype(BF16)
        return jnp.dot(jnp.concatenate([hi, mid], axis=1), e, preferred_element_type=F32)

    planes = lambda o_ref: jnp.concatenate([o_ref[p] for p in range(o_ref.shape[0])], axis=1)
    ya = (expand(e0 / den) * planes(o0_ref) + expand(e1 / den) * planes(o1_ref) + expand(e2 / den) * planes(o2_ref))
    pa = jnp.dot(ya.astype(BF16), wpa_ref[...], preferred_element_type=F32)
    pb = jnp.dot(yb_ref[...], wpb_ref[...], preferred_element_type=F32)
    yc = yc_ref[...]
    glu = yc * jax.nn.sigmoid(jnp.dot(yc.astype(BF16), wglu_ref[...], preferred_element_type=F32) + bglu_ref[...])
    pc = jnp.dot(glu.astype(BF16), wpc_ref[...], preferred_element_type=F32)
    merged = (g0_ref[...].astype(F32) * pa + g1_ref[...].astype(F32) * pb + g2_ref[...].astype(F32) * pc)
    out_ref[...] = merged.astype(out_ref.dtype)


def _resident(shape, index_map):
    return pl.BlockSpec(shape, index_map, pipeline_mode=pl.Buffered(1))


def _merge(outs, lses, yb, yc, gates, expand, w_pa, w_pb, w_pc, w_glu, b_glu, layer, d_model, tm):
    t = yb.shape[0]
    row = lambda w: pl.BlockSpec((tm, w), lambda i: (i, 0))
    full = lambda a, b: _resident((None, a, b), lambda i: (layer, 0, 0))
    return pl.pallas_call(
        _merge_kernel,
        grid=(t // tm,),
        in_specs=[pl.BlockSpec((WIDTH_A // LANES, tm, LANES), lambda i: (0, i, 0))] * 3
        + [row(LANES)] * 3 + [row(WIDTH_B), row(WIDTH_C)]
        + [pl.BlockSpec((tm, d_model), lambda i, k=k: (i, k)) for k in range(N_BRANCH)]
        + [_resident((2 * LANES, WIDTH_A), lambda i: (0, 0)),
           full(WIDTH_A, d_model), full(WIDTH_B, d_model), full(WIDTH_C, d_model),
           full(WIDTH_C, WIDTH_C), full(1, WIDTH_C)],
        out_specs=pl.BlockSpec((tm, d_model), lambda i: (i, 0)),
        out_shape=jax.ShapeDtypeStruct((t, d_model), BF16),
        compiler_params=_params(("arbitrary",)),
        name="merge",
    )(*outs, *lses, yb, yc, gates, gates, gates, expand, w_pa, w_pb, w_pc, w_glu, b_glu)


def _proj_ln_kernel(m_ref, x_ref, w_ref, g_ref, b_ref, o_ref, ob_ref, *, alpha):
    half = m_ref.shape[0] // 2
    for rs in (slice(0, half), slice(half, 2 * half)):
        f = jnp.dot(m_ref[rs, :], w_ref[...], preferred_element_type=F32)
        y = _layer_norm_rows(alpha * x_ref[rs, :] + f, g_ref[...], b_ref[...])
        o_ref[rs, :] = y
        ob_ref[rs, :] = y.astype(BF16)


def _proj_ln(merged, x, w_o, ln_g, ln_b, layer, alpha, tm):
    t, d = x.shape
    return pl.pallas_call(
        functools.partial(_proj_ln_kernel, alpha=alpha),
        grid=(t // tm,),
        in_specs=[
            pl.BlockSpec((tm, d), lambda i: (i, 0)),
            pl.BlockSpec((tm, d), lambda i: (i, 0)),
            _resident((None, d, d), lambda i: (layer, 0, 0)),
            _resident((None, 1, d), lambda i: (layer, 0, 0)),
            _resident((None, 1, d), lambda i: (layer, 0, 0)),
        ],
        out_specs=[pl.BlockSpec((tm, d), lambda i: (i, 0)), pl.BlockSpec((tm, d), lambda i: (i, 0))],
        out_shape=[jax.ShapeDtypeStruct((t, d), F32), jax.ShapeDtypeStruct((t, d), BF16)],
        compiler_params=_params(("arbitrary",)),
        name="out_proj_ln",
    )(merged, x, w_o, ln_g, ln_b)


def _swiglu_kernel(x_ref, wg_ref, wu_ref, o_ref, wgb_ref, wub_ref):
    @pl.when(pl.program_id(1) == 0)
    def _():
        wgb_ref[...] = wg_ref[...].astype(BF16)
        wub_ref[...] = wu_ref[...].astype(BF16)

    x = x_ref[...]
    gate = jnp.dot(x, wgb_ref[...], preferred_element_type=F32)
    up = jnp.dot(x, wub_ref[...], preferred_element_type=F32)
    o_ref[...] = (jax.nn.silu(gate) * up).astype(o_ref.dtype)


def _swiglu(xb, w_in, layer, d_ff, bm, bn):
    t, d = xb.shape
    nj = d_ff // bn
    return pl.pallas_call(
        _swiglu_kernel,
        grid=(nj, t // bm),
        in_specs=[
            pl.BlockSpec((bm, d), lambda j, i: (i, 0)),
            pl.BlockSpec((None, d, bn), lambda j, i: (layer, 0, j)),
            pl.BlockSpec((None, d, bn), lambda j, i: (layer, 0, j + nj)),
        ],
        out_specs=pl.BlockSpec((bm, bn), lambda j, i: (i, j)),
        out_shape=jax.ShapeDtypeStruct((t, d_ff), BF16),
        scratch_shapes=[pltpu.VMEM((d, bn), BF16), pltpu.VMEM((d, bn), BF16)],
        compiler_params=_params(("arbitrary", "arbitrary")),
        name="ffn_in",
    )(xb, w_in, w_in)


def _ffn_out_ln_kernel(h_ref, w_ref, x_ref, g_ref, b_ref, o_ref, ob_ref, *, alpha, nk):
    k = pl.program_id(1)
    part = jnp.dot(h_ref[...], w_ref[...], preferred_element_type=F32)

    @pl.when(k == 0)
    def _():
        o_ref[...] = part

    @pl.when(jnp.logical_and(k > 0, k < nk - 1))
    def _():
        o_ref[...] += part

    @pl.when(k == nk - 1)
    def _():
        y = _layer_norm_rows(alpha * x_ref[...] + (o_ref[...] + part), g_ref[...], b_ref[...])
        o_ref[...] = y
        ob_ref[...] = y.astype(BF16)


def _ffn_out_ln(h, x, w_out, ln_g, ln_b, layer, alpha, bm, bk):
    t, d = x.shape
    d_ff = h.shape[1]
    nk = d_ff // bk
    return pl.pallas_call(
        functools.partial(_ffn_out_ln_kernel, alpha=alpha, nk=nk),
        grid=(t // bm, nk),
        in_specs=[
            pl.BlockSpec((bm, bk), lambda i, k: (i, k)),
            pl.BlockSpec((None, bk, d), lambda i, k: (layer, k, 0)),
            pl.BlockSpec((bm, d), lambda i, k: (i, 0)),
            pl.BlockSpec((None, 1, d), lambda i, k: (layer, 0, 0)),
            pl.BlockSpec((None, 1, d), lambda i, k: (layer, 0, 0)),
        ],
        out_specs=[pl.BlockSpec((bm, d), lambda i, k: (i, 0)), pl.BlockSpec((bm, d), lambda i, k: (i, 0))],
        out_shape=[jax.ShapeDtypeStruct((t, d), F32), jax.ShapeDtypeStruct((t, d), BF16)],
        compiler_params=_params(("arbitrary", "arbitrary"), VMEM_LIMIT_LARGE),
        name="ffn_out_ln",
    )(h, w_out, x, ln_g, ln_b)


@jax.jit
def kernel(x, w_in, b_in, rel_bias, sgu_ln_g, sgu_ln_b, w_s, b_s, lam_re, lam_im, log_dt, b_re, b_im, c_re, c_im, d_skip, w_glu, b_glu, w_pa, w_pb, w_pc, w_o, ln1_g, ln1_b, w_ffn_in, w_ffn_out, ln2_g, ln2_b):
    bsz, seq, d_model = x.shape
    depth = w_in.shape[0]
    d_ff = w_ffn_out.shape[1]
    t = bsz * seq
    alpha = (2 * depth) ** 0.25
    in_cols = w_in.shape[-1]
    gl_width = N_BRANCH * d_model
    zb_src = 3 * QKV_WIDTH
    uc_src = zb_src + 2 * WIDTH_B
    gl_src = uc_src + WIDTH_C
    assert gl_src + gl_width == in_cols
    bn = 768
    zu_width = 2 * WIDTH_B + WIDTH_C
    zb_col, uc_col = 0, 2 * WIDTH_B

    bf = lambda a: a.astype(BF16)
    row3 = lambda a: a.reshape(a.shape[0], 1, a.shape[1]).astype(F32)
    w_pa_b, w_pb_b, w_pc_b, w_glu_b = bf(w_pa), bf(w_pb), bf(w_pc), bf(w_glu)
    w_o_b, w_ffn_out_b = bf(w_o), bf(w_ffn_out)
    w_in32, w_ffn_in32 = w_in.astype(F32), w_ffn_in.astype(F32)
    b_in3, b_glu3, d_skip3 = row3(b_in), row3(b_glu), row3(d_skip)
    sgu_g3, sgu_b3 = row3(sgu_ln_g), row3(sgu_ln_b)
    ln1_g3, ln1_b3, ln2_g3, ln2_b3 = row3(ln1_g), row3(ln1_b), row3(ln2_g), row3(ln2_b)
    w_s32 = w_s.astype(F32)

    group_bias = [_group_rel_bias(rel_bias, g, dil, win // dil) for g, (win, dil) in enumerate(ATT_PATTERNS)]
    expand = jnp.asarray(np.tile(np.kron(np.eye(LANES, HEADS_PER_GROUP, dtype=np.float32),
                                         np.ones((1, HEAD_DIM), np.float32)), (2, 1)), dtype=BF16)

    s5_w, s5_mo, s5_coef = _s5_prep(lam_re, lam_im, log_dt, b_re, b_im, c_re, c_im)

    xf = x.reshape(t, d_model).astype(F32)
    xb = bf(xf)
    for l in range(depth):
        qkv = _in_proj(xb, w_in32, b_in3, l, lambda j: j, 3 * QKV_WIDTH, BF16, TOKEN_TILE, WIDTH_A, "streams")
        zu = _in_proj(xb, w_in32, b_in3, l, lambda j: j + zb_src // bn, zu_width, F32, TOKEN_TILE // 2, bn, "plain")
        gates = _in_proj(xb, w_in32, b_in3, l, lambda j: j + gl_src // bn, gl_width, BF16, TOKEN_TILE, bn,
                         "sigmoid")
        outs, lses = [], []
        for g, (window, dil) in enumerate(ATT_PATTERNS):
            o_g, lse_g = _attention_group(qkv, group_bias[g], g, dil, bsz, seq)
            outs.append(o_g)
            lses.append(lse_g)
        yb = _spatial_gating(zu, zb_col, sgu_g3, sgu_b3, w_s32, b_s, l, 512)
        yc = _s5(zu, uc_col, s5_w, s5_mo, s5_coef, d_skip3, l, bsz, seq)
        merged = _merge(outs, lses, yb, yc, gates, expand, w_pa_b, w_pb_b, w_pc_b, w_glu_b, b_glu3,
                        l, d_model, 512)
        xf, xb = _proj_ln(merged, xf, w_o_b, ln1_g3, ln1_b3, l, alpha, 512)
        h = _swiglu(xb, w_ffn_in32, l, d_ff, 1024, 512)
        xf, xb = _ffn_out_ln(h, xf, w_ffn_out_b, ln2_g3, ln2_b3, l, alpha, 512, d_ff // 2)
    return xf.reshape(bsz, seq, d_model).astype(x.dtype)
```

```python
import functools
import math

import numpy as np
import jax
import jax.numpy as jnp
from jax import lax
from jax.experimental import pallas as pl
from jax.experimental.pallas import tpu as pltpu

F32 = jnp.float32
BF16 = jnp.bfloat16

ATT_PATTERNS = ((128, 1), (512, 4), (2048, 16))
HEADS_PER_GROUP = 8
HEAD_DIM = 64
N_HEADS = len(ATT_PATTERNS) * HEADS_PER_GROUP
QKV_WIDTH = N_HEADS * HEAD_DIM
WIDTH_A = HEADS_PER_GROUP * HEAD_DIM
ATT_BLOCK = 128
N_REL_BUCKETS = 32
REL_MAX_DIST = 2048
NEG_INF = -1e30
CHUNK = 128
WIDTH_B = 768
N_GROUPS_B = 6
WIDTH_C = 768
SSM_GROUP = 16
N_GROUPS_C = WIDTH_C // SSM_GROUP
SSM_STATE = 64
N_BRANCH = 3
LN_EPS = 1e-5

LANES = 128
SUBLANES = 8
VMEM_LIMIT = 48 * 1024 * 1024
VMEM_LIMIT_LARGE = 56 * 1024 * 1024

SLAB = LANES
GROUPS_PER_SLAB = SLAB // SSM_GROUP
N_SLABS = WIDTH_C // SLAB
SLAB_STATES = GROUPS_PER_SLAB * SSM_STATE


def _params(sem, vmem=VMEM_LIMIT):
    return pltpu.CompilerParams(dimension_semantics=sem, vmem_limit_bytes=vmem)


def _layer_norm_rows(y, g, b):
    mu = jnp.mean(y, axis=-1, keepdims=True)
    yc = y - mu
    var = jnp.mean(yc * yc, axis=-1, keepdims=True)
    return yc * lax.rsqrt(var + LN_EPS) * g + b


TOKEN_TILE = 2048
ATTN_UNROLL = 8


def _in_proj_kernel(x_ref, w_ref, b_ref, o_ref, wb_ref, *scratch, mode):
    @pl.when(pl.program_id(1) == 0)
    def _():
        wb_ref[...] = w_ref[...].astype(BF16)

    acc = jnp.dot(x_ref[...], wb_ref[...], preferred_element_type=F32) + b_ref[...]
    if mode == "plain":
        o_ref[...] = acc.astype(o_ref.dtype)
    elif mode == "sigmoid":
        o_ref[...] = (0.5 * jnp.tanh(0.5 * acc) + 0.5).astype(o_ref.dtype)
    else:
        (scr_ref,) = scratch
        grp = pl.program_id(0) % len(ATT_PATTERNS)
        rows = o_ref.shape[0]
        for gi, (_, r) in enumerate(ATT_PATTERNS):
            @pl.when(grp == gi)
            def _(r=r):
                if r == 1:
                    o_ref[...] = acc.astype(o_ref.dtype)
                else:
                    per = rows // r
                    for kt in range(o_ref.shape[1] // LANES):
                        cs = slice(kt * LANES, (kt + 1) * LANES)
                        scr_ref[kt] = acc[:, cs]
                        for c in range(r):
                            o_ref[c * per:(c + 1) * per, cs] = (
                                scr_ref[kt, pl.ds(c, per, stride=r), :].astype(o_ref.dtype))


def _in_proj(x, w, b, layer, col_map, n_out, out_dtype, bm, bn, mode):
    m, k = x.shape
    scratch = [pltpu.VMEM((k, bn), BF16)]
    if mode == "streams":
        scratch.append(pltpu.VMEM((bn // LANES, bm, LANES), F32))
    return pl.pallas_call(
        functools.partial(_in_proj_kernel, mode=mode),
        grid=(n_out // bn, m // bm),
        in_specs=[
            pl.BlockSpec((bm, k), lambda j, i: (i, 0)),
            pl.BlockSpec((None, k, bn), lambda j, i: (layer, 0, col_map(j))),
            pl.BlockSpec((None, 1, bn), lambda j, i: (layer, 0, col_map(j))),
        ],
        out_specs=pl.BlockSpec((bm, bn), lambda j, i: (i, j)),
        out_shape=jax.ShapeDtypeStruct((m, n_out), out_dtype),
        scratch_shapes=scratch,
        compiler_params=_params(("arbitrary", "arbitrary"), VMEM_LIMIT_LARGE),
        name="in_proj_" + mode,
    )(x, w, b)


def _attn_kernel(q_ref, k_ref, v_ref, kp_ref, vp_ref, base_ref, o_ref, lse_ref, kf_ref, vf_ref, bias_ref,
                 *, nblk, r, sg):
    q_blk = ATT_BLOCK
    per = nblk * q_blk
    tile = pl.program_id(1)
    cg = pl.program_id(2)

    @pl.when(jnp.logical_and(pl.program_id(0) == 0, jnp.logical_and(tile == 0, cg == 0)))
    def _():
        col = lax.broadcasted_iota(jnp.int32, (q_blk, 2 * q_blk), 1)
        for h in range(HEADS_PER_GROUP):
            base = jnp.broadcast_to(base_ref[h:h + 1, :], (q_blk, 2 * q_blk))
            tab_h = pltpu.roll(base, 0, 1, stride=1, stride_axis=0)
            bias_ref[h] = tab_h
            bias_ref[HEADS_PER_GROUP + h] = jnp.where(col >= q_blk, tab_h, NEG_INF)

    for st in range(sg):
        dst = st * (per + q_blk)
        kf_ref[dst:dst + q_blk, :] = kp_ref[st * q_blk:(st + 1) * q_blk, :]
        kf_ref[dst + q_blk:dst + q_blk + per, :] = k_ref[st * per:(st + 1) * per, :]
        vf_ref[dst:dst + q_blk, :] = vp_ref[st * q_blk:(st + 1) * q_blk, :]
        vf_ref[dst + q_blk:dst + q_blk + per, :] = v_ref[st * per:(st + 1) * per, :]
    lane = lax.broadcasted_iota(jnp.int32, (q_blk, LANES), 1)
    lo = lane < HEAD_DIM
    scale = HEAD_DIM ** -0.5

    def body(u, carry):
        st, n = (0, u) if sg == 1 else (u, 0)
        r0 = pl.multiple_of(u * q_blk, q_blk)
        k0 = pl.multiple_of((u + st) * q_blk, q_blk)
        out_rows = pl.ds(r0, q_blk) if r == 1 else pl.ds(n * (q_blk * r) + cg * sg + st, q_blk, stride=r)
        tab = jnp.where(jnp.logical_and(tile == 0, n == 0), HEADS_PER_GROUP, 0)
        lse_tile = jnp.zeros((q_blk, LANES), F32)
        for pr in range(HEADS_PER_GROUP // 2):
            cs = slice(pr * LANES, (pr + 1) * LANES)
            q2 = q_ref[pl.ds(r0, q_blk), cs]
            k2 = kf_ref[pl.ds(k0, 2 * q_blk), cs]
            v2 = vf_ref[pl.ds(k0, 2 * q_blk), cs]
            outs = []
            for hh in range(2):
                h = 2 * pr + hh
                sel = lo if hh == 0 else jnp.logical_not(lo)
                qm = jnp.where(sel, q2 * scale, jnp.zeros_like(q2))
                s = lax.dot_general(qm, k2, (((1,), (1,)), ((), ())), preferred_element_type=F32)
                s = s + bias_ref[tab + h]
                m = jnp.max(s, axis=-1, keepdims=True)
                p = jnp.exp(s - m)
                den = jnp.sum(p, axis=-1, keepdims=True)
                o = jnp.dot(p.astype(BF16), v2, preferred_element_type=F32) / den
                outs.append(o)
                lse_tile = jnp.where(lane == h, m + jnp.log(den), lse_tile)
            o_ref[pr, out_rows, :] = jnp.where(lo, outs[0], outs[1]).astype(o_ref.dtype)
        lse_ref[out_rows, :] = lse_tile
        return carry

    lax.fori_loop(0, sg * nblk, body, 0, unroll=min(sg * nblk, ATTN_UNROLL))


def _attention_group(qkv, bias, g, dilation, bsz, seq):
    r = dilation
    assert seq % TOKEN_TILE == 0 and TOKEN_TILE % (r * ATT_BLOCK) == 0
    ntile = seq // TOKEN_TILE
    per = TOKEN_TILE // r
    nblk = per // ATT_BLOCK
    sg = min(r, ATTN_UNROLL) if nblk == 1 else 1
    ncg = r // sg
    gpw = QKV_WIDTH // WIDTH_A

    def cur(which):
        return lambda b, t, c: ((b * ntile + t) * ncg + c, which * gpw + g)

    def prev(which):
        return lambda b, t, c: (((b * ntile + jnp.maximum(t - 1, 0)) * ncg + c) * nblk + nblk - 1, which * gpw + g)

    return pl.pallas_call(
        functools.partial(_attn_kernel, nblk=nblk, r=r, sg=sg),
        grid=(bsz, ntile, ncg),
        in_specs=[
            pl.BlockSpec((sg * per, WIDTH_A), cur(0)),
            pl.BlockSpec((sg * per, WIDTH_A), cur(1)),
            pl.BlockSpec((sg * per, WIDTH_A), cur(2)),
            pl.BlockSpec((sg * ATT_BLOCK, WIDTH_A), prev(1)),
            pl.BlockSpec((sg * ATT_BLOCK, WIDTH_A), prev(2)),
            _resident((HEADS_PER_GROUP, 2 * ATT_BLOCK), lambda b, t, c: (0, 0)),
        ],
        out_specs=[
            pl.BlockSpec((WIDTH_A // LANES, TOKEN_TILE, LANES), lambda b, t, c: (0, b * ntile + t, 0)),
            pl.BlockSpec((TOKEN_TILE, LANES), lambda b, t, c: (b * ntile + t, 0)),
        ],
        out_shape=[
            jax.ShapeDtypeStruct((WIDTH_A // LANES, bsz * seq, LANES), F32),
            jax.ShapeDtypeStruct((bsz * seq, LANES), F32),
        ],
        scratch_shapes=[
            pltpu.VMEM((sg * (per + ATT_BLOCK), WIDTH_A), BF16),
            pltpu.VMEM((sg * (per + ATT_BLOCK), WIDTH_A), BF16),
            pltpu.VMEM((2 * HEADS_PER_GROUP, ATT_BLOCK, 2 * ATT_BLOCK), F32),
        ],
        compiler_params=_params(("arbitrary", "arbitrary", "arbitrary")),
        name=f"attn_d{r}",
    )(qkv, qkv, qkv, qkv, qkv, bias)


def _t5_bucket(dist):
    max_exact = N_REL_BUCKETS // 2
    d = np.maximum(dist, 1).astype(np.float32)
    scale = (N_REL_BUCKETS - max_exact) / math.log(REL_MAX_DIST / max_exact)
    large = max_exact + (np.log(d / max_exact) * scale).astype(np.int32)
    large = np.minimum(large, N_REL_BUCKETS - 1)
    return np.where(dist < max_exact, dist, large).astype(np.int32)


def _group_rel_bias(rel_bias, g, dilation, n_steps):
    assert n_steps <= ATT_BLOCK
    dist = (ATT_BLOCK - np.arange(2 * ATT_BLOCK)) % (2 * ATT_BLOCK)
    cols = rel_bias[:, g * HEADS_PER_GROUP:(g + 1) * HEADS_PER_GROUP]
    vals = jnp.transpose(cols[_t5_bucket(dist * dilation)]).astype(F32)
    return jnp.where((dist <= n_steps)[None], vals, NEG_INF)


def _gmlp_kernel(x_ref, wz_ref, bz_ref, g_ref, b_ref, w_ref, bs_ref, o_ref, wzb_ref, wsb_ref, *, nsub):
    @pl.when(pl.program_id(0) == 0)
    def _():
        wzb_ref[...] = wz_ref[...].astype(BF16)
        row = lax.broadcasted_iota(jnp.int32, (CHUNK, CHUNK), 0)
        col = lax.broadcasted_iota(jnp.int32, (CHUNK, CHUNK), 1)
        for g in range(N_GROUPS_B):
            wsb_ref[g] = jnp.where(row >= col, w_ref[g], 0.0).astype(BF16)

    sub = x_ref.shape[0] // nsub
    nchunk = sub // CHUNK
    bs = bs_ref[...]
    for si in range(nsub):
        r0 = si * sub
        z = jnp.dot(x_ref[r0:r0 + sub, :], wzb_ref[...], preferred_element_type=F32) + bz_ref[...]
        z = jax.nn.gelu(z)
        u = z[:, :WIDTH_B]
        v = _layer_norm_rows(z[:, WIDTH_B:], g_ref[...], b_ref[...]).astype(BF16)
        for g in range(N_GROUPS_B):
            cs = slice(g * LANES, (g + 1) * LANES)
            rhs = jnp.concatenate([v[n * CHUNK:(n + 1) * CHUNK, cs] for n in range(nchunk)], axis=1)
            mixed = jnp.dot(wsb_ref[g], rhs, preferred_element_type=F32)
            for n in range(nchunk):
                rs = slice(n * CHUNK, (n + 1) * CHUNK)
                o_ref[r0 + n * CHUNK:r0 + (n + 1) * CHUNK, cs] = (
                    u[rs, cs] * (mixed[:, n * LANES:(n + 1) * LANES] + bs[:, cs])).astype(o_ref.dtype)


def _spatial_gating(xb, w_in, b_in, zb_col, ln_g, ln_b, w_s, b_s, layer, tm, nsub):
    t, d = xb.shape
    zw = 2 * WIDTH_B
    assert zb_col % zw == 0
    bs = jnp.repeat(jnp.transpose(b_s[layer]), LANES, axis=1)
    return pl.pallas_call(
        functools.partial(_gmlp_kernel, nsub=nsub),
        grid=(t // tm,),
        in_specs=[
            pl.BlockSpec((tm, d), lambda i: (i, 0)),
            _resident((None, d, zw), lambda i: (layer, 0, zb_col // zw)),
            _resident((None, 1, zw), lambda i: (layer, 0, zb_col // zw)),
            _resident((None, 1, WIDTH_B), lambda i: (layer, 0, 0)),
            _resident((None, 1, WIDTH_B), lambda i: (layer, 0, 0)),
            _resident((None, N_GROUPS_B, CHUNK, CHUNK), lambda i: (layer, 0, 0, 0)),
            _resident((CHUNK, WIDTH_B), lambda i: (0, 0)),
        ],
        out_specs=pl.BlockSpec((tm, WIDTH_B), lambda i: (i, 0)),
        out_shape=jax.ShapeDtypeStruct((t, WIDTH_B), BF16),
        scratch_shapes=[pltpu.VMEM((d, zw), BF16), pltpu.VMEM((N_GROUPS_B, CHUNK, CHUNK), BF16)],
        compiler_params=_params(("arbitrary",)),
        name="gmlp",
    )(xb, w_in, b_in, ln_g, ln_b, w_s, bs)


S5_CHUNK = 8
CHUNK_LANES = S5_CHUNK * SLAB
STATE_LANES = 2 * SLAB_STATES
N_POWERS = 2 * SUBLANES


def _s5_prep_kernel(lr_ref, li_ref, dt_ref, bre_ref, bim_ref, cre_ref, cim_ref, w_ref, mo_ref, coef_ref):
    nl, ns = S5_CHUNK, SLAB_STATES
    lr = lr_ref[...]
    li = li_ref[...]
    dt = jnp.exp(dt_ref[...])

    def apow(k):
        mag = jnp.exp(lr * dt * k)
        ang = li * dt * k
        return mag * jnp.cos(ang), mag * jnp.sin(ang)

    pr, pi = apow(lax.broadcasted_iota(jnp.int32, (N_POWERS, 1), 0).astype(F32))
    ab_re, ab_im = pr[1:2], pi[1:2]
    nrm = lr * lr + li * li
    zr = ((ab_re - 1.0) * lr + ab_im * li) / nrm
    zi = (ab_im * lr - (ab_re - 1.0) * li) / nrm
    dr = pr * zr - pi * zi
    di = pr * zi + pi * zr
    pr_c, pi_c, dr_c, di_c = pr.T, pi.T, dr.T, di.T
    bre, bim = bre_ref[...], bim_ref[...]
    cre, cim = cre_ref[...], cim_ref[...]
    mo_ref[0:CHUNK_LANES, :] = jnp.zeros((CHUNK_LANES, CHUNK_LANES), mo_ref.dtype)
    for e in range(nl):
        j = nl - 1 - e
        rows = slice(j * SLAB, (j + 1) * SLAB)
        w_ref[rows, 0:ns] = (bre * dr[e:e + 1] - bim * di[e:e + 1]).astype(w_ref.dtype)
        w_ref[rows, ns:2 * ns] = (bre * di[e:e + 1] + bim * dr[e:e + 1]).astype(w_ref.dtype)
        cdr = cre * dr_c[:, e:e + 1] - cim * di_c[:, e:e + 1]
        cdi = cre * di_c[:, e:e + 1] + cim * dr_c[:, e:e + 1]
        kt = (jnp.dot(bre, cdr, preferred_element_type=F32, precision=lax.Precision.HIGHEST)
              - jnp.dot(bim, cdi, preferred_element_type=F32, precision=lax.Precision.HIGHEST)).astype(mo_ref.dtype)
        for jj in range(nl - e):
            mo_ref[jj * SLAB:(jj + 1) * SLAB, (jj + e) * SLAB:(jj + e + 1) * SLAB] = kt
        cols = slice(e * SLAB, (e + 1) * SLAB)
        qr, qi = pr_c[:, e + 1:e + 2], pi_c[:, e + 1:e + 2]
        mo_ref[CHUNK_LANES:CHUNK_LANES + ns, cols] = (cre * qr - cim * qi).astype(mo_ref.dtype)
        mo_ref[CHUNK_LANES + ns:CHUNK_LANES + 2 * ns, cols] = (-(cre * qi + cim * qr)).astype(mo_ref.dtype)
    ar, ai = apow(((lax.broadcasted_iota(jnp.int32, (SUBLANES, 1), 0) + 1) * nl).astype(F32))
    rowi = lax.broadcasted_iota(jnp.int32, (SUBLANES, ns), 0)
    for idx, kk in enumerate((1, 2, 4)):
        coef_ref[2 * idx] = jnp.where(rowi >= kk, ar[kk - 1:kk], 0.0)
        coef_ref[2 * idx + 1] = jnp.where(rowi >= kk, ai[kk - 1:kk], 0.0)
    coef_ref[6] = ar
    coef_ref[7] = ai


def _s5_prep(lam_re, lam_im, log_dt, b_re, b_im, c_re, c_im):
    depth = lam_re.shape[0]
    g8, p, h = GROUPS_PER_SLAB, SSM_STATE, SSM_GROUP
    eye = jnp.eye(g8, dtype=F32)
    row = lambda a: a.astype(F32).reshape(depth, N_SLABS, 1, SLAB_STATES)
    dt = row(jnp.repeat(log_dt.astype(F32)[..., None], p, axis=-1))

    def b_diag(b):
        b = b.astype(F32).reshape(depth, N_SLABS, g8, p, h)
        return jnp.einsum("dsgph,gk->dsghkp", b, eye).reshape(depth, N_SLABS, SLAB, SLAB_STATES)

    def c_diag(c):
        c = c.astype(F32).reshape(depth, N_SLABS, g8, h, p)
        return jnp.einsum("dsghp,gk->dsgpkh", c, eye).reshape(depth, N_SLABS, SLAB_STATES, SLAB)

    blk = lambda a, b: pl.BlockSpec((None, None, a, b), lambda d, s: (d, s, 0, 0))
    return pl.pallas_call(
        _s5_prep_kernel,
        grid=(depth, N_SLABS),
        in_specs=[blk(1, SLAB_STATES)] * 3 + [blk(SLAB, SLAB_STATES)] * 2 + [blk(SLAB_STATES, SLAB)] * 2,
        out_specs=[blk(CHUNK_LANES, STATE_LANES), blk(CHUNK_LANES + STATE_LANES, CHUNK_LANES),
                   pl.BlockSpec((None, None, 8, SUBLANES, SLAB_STATES), lambda d, s: (d, s, 0, 0, 0))],
        out_shape=[jax.ShapeDtypeStruct((depth, N_SLABS, CHUNK_LANES, STATE_LANES), BF16),
                   jax.ShapeDtypeStruct((depth, N_SLABS, CHUNK_LANES + STATE_LANES, CHUNK_LANES), BF16),
                   jax.ShapeDtypeStruct((depth, N_SLABS, 8, SUBLANES, SLAB_STATES), F32)],
        compiler_params=_params(("arbitrary", "arbitrary")),
        name="s5_prep",
    )(row(lam_re), row(lam_im), dt, b_diag(b_re), b_diag(b_im), c_diag(c_re), c_diag(c_im))


def _s5_kernel(u_ref, w_ref, mo_ref, coef_ref, d_ref, o_ref, ubuf_ref):
    nl = S5_CHUNK
    seq = u_ref.shape[0]
    n = seq // nl
    nv = n // SUBLANES
    ns = SLAB_STATES
    ubuf_ref[0:nl, :] = jnp.zeros((nl, SLAB), F32)
    ubuf_ref[nl:, :] = u_ref[...]
    cur = [ubuf_ref[pl.ds(nl + j, n, stride=nl), :] for j in range(nl)]
    prev = [ubuf_ref[pl.ds(j, n, stride=nl), :] for j in range(nl)]
    u2 = jnp.concatenate([t.astype(BF16) for t in cur], axis=1)
    u2p = jnp.concatenate([t.astype(BF16) for t in prev], axis=1)
    sc = jnp.dot(u2p, w_ref[...], preferred_element_type=F32)
    xr = sc[:, :ns].reshape(nv, SUBLANES, ns)
    xi = sc[:, ns:].reshape(nv, SUBLANES, ns)
    for idx, k in enumerate((1, 2, 4)):
        cr = coef_ref[2 * idx]
        ci = coef_ref[2 * idx + 1]
        sr = pltpu.roll(xr, k, 1)
        si = pltpu.roll(xi, k, 1)
        xr, xi = xr + (cr * sr - ci * si), xi + (cr * si + ci * sr)
    pr = coef_ref[6]
    pi = coef_ref[7]
    c_r = jnp.zeros((1, ns), F32)
    c_i = jnp.zeros((1, ns), F32)
    out_r, out_i = [], []
    for r in range(nv):
        yr = xr[r] + (pr * c_r - pi * c_i)
        yi = xi[r] + (pr * c_i + pi * c_r)
        out_r.append(yr)
        out_i.append(yi)
        c_r = yr[SUBLANES - 1:SUBLANES, :]
        c_i = yi[SUBLANES - 1:SUBLANES, :]
    x = jnp.concatenate([jnp.concatenate(out_r, axis=0), jnp.concatenate(out_i, axis=0)], axis=1)
    lhs = jnp.concatenate([u2, x.astype(BF16)], axis=1)
    y = jnp.dot(lhs, mo_ref[...], preferred_element_type=F32)
    d = d_ref[...]
    for j in range(nl):
        o_ref[pl.ds(j, n, stride=nl), :] = jax.nn.gelu(y[:, j * SLAB:(j + 1) * SLAB] + d * cur[j]).astype(o_ref.dtype)


def _s5(zu, uc_col, wmat, mo, coef, d_skip, layer, bsz, seq):
    assert seq % (S5_CHUNK * SUBLANES) == 0
    zu3 = zu.reshape(bsz, seq, zu.shape[-1])
    c0 = uc_col // SLAB
    out = pl.pallas_call(
        _s5_kernel,
        grid=(N_SLABS, bsz),
        in_specs=[
            pl.BlockSpec((None, seq, SLAB), lambda s, b: (b, 0, c0 + s)),
            pl.BlockSpec((None, None, CHUNK_LANES, STATE_LANES), lambda s, b: (layer, s, 0, 0)),
            pl.BlockSpec((None, None, CHUNK_LANES + STATE_LANES, CHUNK_LANES), lambda s, b: (layer, s, 0, 0)),
            pl.BlockSpec((None, None, 8, SUBLANES, SLAB_STATES), lambda s, b: (layer, s, 0, 0, 0)),
            pl.BlockSpec((None, 1, SLAB), lambda s, b: (layer, 0, s)),
        ],
        out_specs=pl.BlockSpec((None, seq, SLAB), lambda s, b: (b, 0, s)),
        out_shape=jax.ShapeDtypeStruct((bsz, seq, WIDTH_C), F32),
        scratch_shapes=[pltpu.VMEM((seq + S5_CHUNK, SLAB), F32)],
        compiler_params=_params(("arbitrary", "arbitrary")),
        name="s5_scan",
    )(zu3, wmat, mo, coef, d_skip)
    return out.reshape(bsz * seq, WIDTH_C)


def _merge_kernel(o0_ref, o1_ref, o2_ref, l0_ref, l1_ref, l2_ref, yb_ref, yc_ref,
                  g0_ref, g1_ref, g2_ref, e_ref, wpa_ref, wpb_ref, wpc_ref, wglu_ref, bglu_ref, out_ref):
    l0, l1, l2 = l0_ref[...], l1_ref[...], l2_ref[...]
    m = jnp.maximum(jnp.maximum(l0, l1), l2)
    e0, e1, e2 = jnp.exp(l0 - m), jnp.exp(l1 - m), jnp.exp(l2 - m)
    den = e0 + e1 + e2
    e = e_ref[...]

    def expand(w):
        hi = w.astype(BF16)
        mid = (w - hi.astype(F32)).astype(BF16)
        return jnp.dot(jnp.concatenate([hi, mid], axis=1), e, preferred_element_type=F32)

    planes = lambda o_ref: jnp.concatenate([o_ref[p] for p in range(o_ref.shape[0])], axis=1)
    ya = (expand(e0 / den) * planes(o0_ref) + expand(e1 / den) * planes(o1_ref) + expand(e2 / den) * planes(o2_ref))
    pa = jnp.dot(ya.astype(BF16), wpa_ref[...], preferred_element_type=F32)
    pb = jnp.dot(yb_ref[...], wpb_ref[...], preferred_element_type=F32)
    yc = yc_ref[...]
    glu = yc * jax.nn.sigmoid(jnp.dot(yc.astype(BF16), wglu_ref[...], preferred_element_type=F32) + bglu_ref[...])
    pc = jnp.dot(glu.astype(BF16), wpc_ref[...], preferred_element_type=F32)
    merged = (g0_ref[...].astype(F32) * pa + g1_ref[...].astype(F32) * pb + g2_ref[...].astype(F32) * pc)
    out_ref[...] = merged.astype(out_ref.dtype)


def _resident(shape, index_map):
    return pl.BlockSpec(shape, index_map, pipeline_mode=pl.Buffered(1))


def _merge(outs, lses, yb, yc, gates, expand, w_pa, w_pb, w_pc, w_glu, b_glu, layer, d_model, tm):
    t = yb.shape[0]
    row = lambda w: pl.BlockSpec((tm, w), lambda i: (i, 0))
    full = lambda a, b: _resident((None, a, b), lambda i: (layer, 0, 0))
    return pl.pallas_call(
        _merge_kernel,
        grid=(t // tm,),
        in_specs=[pl.BlockSpec((WIDTH_A // LANES, tm, LANES), lambda i: (0, i, 0))] * 3
        + [row(LANES)] * 3 + [row(WIDTH_B), row(WIDTH_C)]
        + [pl.BlockSpec((tm, d_model), lambda i, k=k: (i, k)) for k in range(N_BRANCH)]
        + [_resident((2 * LANES, WIDTH_A), lambda i: (0, 0)),
           full(WIDTH_A, d_model), full(WIDTH_B, d_model), full(WIDTH_C, d_model),
           full(WIDTH_C, WIDTH_C), full(1, WIDTH_C)],
        out_specs=pl.BlockSpec((tm, d_model), lambda i: (i, 0)),
        out_shape=jax.ShapeDtypeStruct((t, d_model), BF16),
        compiler_params=_params(("arbitrary",)),
        name="merge",
    )(*outs, *lses, yb, yc, gates, gates, gates, expand, w_pa, w_pb, w_pc, w_glu, b_glu)


def _proj_ln_kernel(m_ref, x_ref, w_ref, g_ref, b_ref, o_ref, ob_ref, *, alpha):
    half = m_ref.shape[0] // 2
    for rs in (slice(0, half), slice(half, 2 * half)):
        f = jnp.dot(m_ref[rs, :], w_ref[...], preferred_element_type=F32)
        y = _layer_norm_rows(alpha * x_ref[rs, :] + f, g_ref[...], b_ref[...])
        o_ref[rs, :] = y
        ob_ref[rs, :] = y.astype(BF16)


def _proj_ln(merged, x, w_o, ln_g, ln_b, layer, alpha, tm):
    t, d = x.shape
    return pl.pallas_call(
        functools.partial(_proj_ln_kernel, alpha=alpha),
        grid=(t // tm,),
        in_specs=[
            pl.BlockSpec((tm, d), lambda i: (i, 0)),
            pl.BlockSpec((tm, d), lambda i: (i, 0)),
            _resident((None, d, d), lambda i: (layer, 0, 0)),
            _resident((None, 1, d), lambda i: (layer, 0, 0)),
            _resident((None, 1, d), lambda i: (layer, 0, 0)),
        ],
        out_specs=[pl.BlockSpec((tm, d), lambda i: (i, 0)), pl.BlockSpec((tm, d), lambda i: (i, 0))],
        out_shape=[jax.ShapeDtypeStruct((t, d), F32), jax.ShapeDtypeStruct((t, d), BF16)],
        compiler_params=_params(("arbitrary",)),
        name="out_proj_ln",
    )(merged, x, w_o, ln_g, ln_b)


def _swiglu_kernel(x_ref, wg_ref, wu_ref, o_ref, wgb_ref, wub_ref):
    @pl.when(pl.program_id(1) == 0)
    def _():
        wgb_ref[...] = wg_ref[...].astype(BF16)
        wub_ref[...] = wu_ref[...].astype(BF16)

    x = x_ref[...]
    gate = jnp.dot(x, wgb_ref[...], preferred_element_type=F32)
    up = jnp.dot(x, wub_ref[...], preferred_element_type=F32)
    o_ref[...] = (jax.nn.silu(gate) * up).astype(o_ref.dtype)


def _swiglu(xb, w_in, layer, d_ff, bm, bn):
    t, d = xb.shape
    nj = d_ff // bn
    return pl.pallas_call(
        _swiglu_kernel,
        grid=(nj, t // bm),
        in_specs=[
            pl.BlockSpec((bm, d), lambda j, i: (i, 0)),
            pl.BlockSpec((None, d, bn), lambda j, i: (layer, 0, j)),
            pl.BlockSpec((None, d, bn), lambda j, i: (layer, 0, j + nj)),
        ],
        out_specs=pl.BlockSpec((bm, bn), lambda j, i: (i, j)),
        out_shape=jax.ShapeDtypeStruct((t, d_ff), BF16),
        scratch_shapes=[pltpu.VMEM((d, bn), BF16), pltpu.VMEM((d, bn), BF16)],
        compiler_params=_params(("arbitrary", "arbitrary")),
        name="ffn_in",
    )(xb, w_in, w_in)


def _ffn_out_ln_kernel(h_ref, w_ref, x_ref, g_ref, b_ref, o_ref, ob_ref, *, alpha, nk):
    k = pl.program_id(1)

    @pl.when(k == 0)
    def _():
        o_ref[...] = jnp.dot(h_ref[...], w_ref[...], preferred_element_type=F32)

    @pl.when(jnp.logical_and(k > 0, k < nk - 1))
    def _():
        o_ref[...] += jnp.dot(h_ref[...], w_ref[...], preferred_element_type=F32)

    @pl.when(k == nk - 1)
    def _():
        half = h_ref.shape[0] // 2
        for rs in (slice(0, half), slice(half, 2 * half)):
            part = jnp.dot(h_ref[rs, :], w_ref[...], preferred_element_type=F32)
            y = _layer_norm_rows(alpha * x_ref[rs, :] + (o_ref[rs, :] + part), g_ref[...], b_ref[...])
            o_ref[rs, :] = y
            ob_ref[rs, :] = y.astype(BF16)


def _ffn_out_ln(h, x, w_out, ln_g, ln_b, layer, alpha, bm, bk):
    t, d = x.shape
    d_ff = h.shape[1]
    nk = d_ff // bk
    return pl.pallas_call(
        functools.partial(_ffn_out_ln_kernel, alpha=alpha, nk=nk),
        grid=(t // bm, nk),
        in_specs=[
            pl.BlockSpec((bm, bk), lambda i, k: (i, k)),
            pl.BlockSpec((None, bk, d), lambda i, k: (layer, k, 0)),
            pl.BlockSpec((bm, d), lambda i, k: (i, 0)),
            pl.BlockSpec((None, 1, d), lambda i, k: (layer, 0, 0)),
            pl.BlockSpec((None, 1, d), lambda i, k: (layer, 0, 0)),
        ],
        out_specs=[pl.BlockSpec((bm, d), lambda i, k: (i, 0)), pl.BlockSpec((bm, d), lambda i, k: (i, 0))],
        out_shape=[jax.ShapeDtypeStruct((t, d), F32), jax.ShapeDtypeStruct((t, d), BF16)],
        compiler_params=_params(("arbitrary", "arbitrary"), VMEM_LIMIT_LARGE),
        name="ffn_out_ln",
    )(h, w_out, x, ln_g, ln_b)


@jax.jit
def kernel(x, w_in, b_in, rel_bias, sgu_ln_g, sgu_ln_b, w_s, b_s, lam_re, lam_im, log_dt, b_re, b_im, c_re, c_im, d_skip, w_glu, b_glu, w_pa, w_pb, w_pc, w_o, ln1_g, ln1_b, w_ffn_in, w_ffn_out, ln2_g, ln2_b):
    bsz, seq, d_model = x.shape
    depth = w_in.shape[0]
    d_ff = w_ffn_out.shape[1]
    t = bsz * seq
    alpha = (2 * depth) ** 0.25
    in_cols = w_in.shape[-1]
    gl_width = N_BRANCH * d_model
    zb_src = 3 * QKV_WIDTH
    uc_src = zb_src + 2 * WIDTH_B
    gl_src = uc_src + WIDTH_C
    assert gl_src + gl_width == in_cols
    bn = 768

    bf = lambda a: a.astype(BF16)
    row3 = lambda a: a.reshape(a.shape[0], 1, a.shape[1]).astype(F32)
    w_pa_b, w_pb_b, w_pc_b, w_glu_b = bf(w_pa), bf(w_pb), bf(w_pc), bf(w_glu)
    w_o_b, w_ffn_out_b = bf(w_o), bf(w_ffn_out)
    w_in32, w_ffn_in32 = w_in.astype(F32), w_ffn_in.astype(F32)
    b_in3, b_glu3, d_skip3 = row3(b_in), row3(b_glu), row3(d_skip)
    sgu_g3, sgu_b3 = row3(sgu_ln_g), row3(sgu_ln_b)
    ln1_g3, ln1_b3, ln2_g3, ln2_b3 = row3(ln1_g), row3(ln1_b), row3(ln2_g), row3(ln2_b)
    w_s32 = w_s.astype(F32)

    group_bias = [_group_rel_bias(rel_bias, g, dil, win // dil) for g, (win, dil) in enumerate(ATT_PATTERNS)]
    expand = jnp.asarray(np.tile(np.kron(np.eye(LANES, HEADS_PER_GROUP, dtype=np.float32),
                                         np.ones((1, HEAD_DIM), np.float32)), (2, 1)), dtype=BF16)

    s5_w, s5_mo, s5_coef = _s5_prep(lam_re, lam_im, log_dt, b_re, b_im, c_re, c_im)

    xf = x.reshape(t, d_model).astype(F32)
    xb = bf(xf)
    for l in range(depth):
        qkv = _in_proj(xb, w_in32, b_in3, l, lambda j: j, 3 * QKV_WIDTH, BF16, TOKEN_TILE, WIDTH_A, "streams")
        uc = _in_proj(xb, w_in32, b_in3, l, lambda j: j + uc_src // bn, WIDTH_C, F32, TOKEN_TILE // 2, bn, "plain")
        gates = _in_proj(xb, w_in32, b_in3, l, lambda j: j + gl_src // bn, gl_width, BF16, TOKEN_TILE, bn,
                         "sigmoid")
        outs, lses = [], []
        for g, (window, dil) in enumerate(ATT_PATTERNS):
            o_g, lse_g = _attention_group(qkv, group_bias[g], g, dil, bsz, seq)
            outs.append(o_g)
            lses.append(lse_g)
        yb = _spatial_gating(xb, w_in32, b_in3, zb_src, sgu_g3, sgu_b3, w_s32, b_s, l, 1024, 2)
        yc = _s5(uc, 0, s5_w, s5_mo, s5_coef, d_skip3, l, bsz, seq)
        merged = _merge(outs, lses, yb, yc, gates, expand, w_pa_b, w_pb_b, w_pc_b, w_glu_b, b_glu3,
                        l, d_model, 512)
        xf, xb = _proj_ln(merged, xf, w_o_b, ln1_g3, ln1_b3, l, alpha, 512)
        h = _swiglu(xb, w_ffn_in32, l, d_ff, 1024, 512)
        xf, xb = _ffn_out_ln(h, xf, w_ffn_out_b, ln2_g3, ln2_b3, l, alpha, 512, d_ff // 2)
    return xf.reshape(bsz, seq, d_model).astype(x.dtype)
```

```python
import functools
import math

import numpy as np
import jax
import jax.numpy as jnp
from jax import lax
from jax.experimental import pallas as pl
from jax.experimental.pallas import tpu as pltpu

F32 = jnp.float32
BF16 = jnp.bfloat16

ATT_PATTERNS = ((128, 1), (512, 4), (2048, 16))
HEADS_PER_GROUP = 8
HEAD_DIM = 64
N_HEADS = len(ATT_PATTERNS) * HEADS_PER_GROUP
QKV_WIDTH = N_HEADS * HEAD_DIM
WIDTH_A = HEADS_PER_GROUP * HEAD_DIM
ATT_BLOCK = 128
N_REL_BUCKETS = 32
REL_MAX_DIST = 2048
NEG_INF = -1e30
CHUNK = 128
WIDTH_B = 768
N_GROUPS_B = 6
WIDTH_C = 768
SSM_GROUP = 16
N_GROUPS_C = WIDTH_C // SSM_GROUP
SSM_STATE = 64
N_BRANCH = 3
LN_EPS = 1e-5

LANES = 128
SUBLANES = 8
VMEM_LIMIT = 48 * 1024 * 1024
VMEM_LIMIT_LARGE = 56 * 1024 * 1024

SLAB = LANES
GROUPS_PER_SLAB = SLAB // SSM_GROUP
N_SLABS = WIDTH_C // SLAB
SLAB_STATES = GROUPS_PER_SLAB * SSM_STATE


def _params(sem, vmem=VMEM_LIMIT):
    return pltpu.CompilerParams(dimension_semantics=sem, vmem_limit_bytes=vmem)


def _layer_norm_rows(y, g, b):
    mu = jnp.mean(y, axis=-1, keepdims=True)
    yc = y - mu
    var = jnp.mean(yc * yc, axis=-1, keepdims=True)
    return yc * lax.rsqrt(var + LN_EPS) * g + b


TOKEN_TILE = 2048
ATTN_UNROLL = 8


def _in_proj_kernel(x_ref, w_ref, b_ref, o_ref, wb_ref, *scratch, mode):
    @pl.when(pl.program_id(1) == 0)
    def _():
        wb_ref[...] = w_ref[...].astype(BF16)

    rows = o_ref.shape[0]
    half = rows // 2
    if mode != "streams":
        for rs in (slice(0, half), slice(half, rows)):
            acc = jnp.dot(x_ref[rs, :], wb_ref[...], preferred_element_type=F32) + b_ref[...]
            if mode == "plain":
                o_ref[rs, :] = acc.astype(o_ref.dtype)
            else:
                o_ref[rs, :] = (0.5 * jnp.tanh(0.5 * acc) + 0.5).astype(o_ref.dtype)
        return
    (scr_ref,) = scratch
    grp = pl.program_id(0) % len(ATT_PATTERNS)
    for gi, (_, r) in enumerate(ATT_PATTERNS):
        @pl.when(grp == gi)
        def _(r=r):
            for hf in range(2):
                acc = jnp.dot(x_ref[hf * half:(hf + 1) * half, :], wb_ref[...],
                              preferred_element_type=F32) + b_ref[...]
                if r == 1:
                    o_ref[hf * half:(hf + 1) * half, :] = acc.astype(o_ref.dtype)
                    continue
                per = rows // r
                seg = half // r
                for kt in range(o_ref.shape[1] // LANES):
                    cs = slice(kt * LANES, (kt + 1) * LANES)
                    scr_ref[hf, kt] = acc[:, cs]
                    for c in range(r):
                        o_ref[c * per + hf * seg:c * per + (hf + 1) * seg, cs] = (
                            scr_ref[hf, kt, pl.ds(c, seg, stride=r), :].astype(o_ref.dtype))


def _in_proj(x, w, b, layer, col_map, n_out, out_dtype, bm, bn, mode):
    m, k = x.shape
    scratch = [pltpu.VMEM((k, bn), BF16)]
    if mode == "streams":
        scratch.append(pltpu.VMEM((2, bn // LANES, bm // 2, LANES), F32))
    return pl.pallas_call(
        functools.partial(_in_proj_kernel, mode=mode),
        grid=(n_out // bn, m // bm),
        in_specs=[
            pl.BlockSpec((bm, k), lambda j, i: (i, 0)),
            pl.BlockSpec((None, k, bn), lambda j, i: (layer, 0, col_map(j))),
            pl.BlockSpec((None, 1, bn), lambda j, i: (layer, 0, col_map(j))),
        ],
        out_specs=pl.BlockSpec((bm, bn), lambda j, i: (i, j)),
        out_shape=jax.ShapeDtypeStruct((m, n_out), out_dtype),
        scratch_shapes=scratch,
        compiler_params=_params(("arbitrary", "arbitrary"), VMEM_LIMIT_LARGE),
        name="in_proj_" + mode,
    )(x, w, b)


def _attn_kernel(q_ref, k_ref, v_ref, kp_ref, vp_ref, base_ref, o_ref, lse_ref, kf_ref, vf_ref, bias_ref,
                 *, nblk, r, sg):
    q_blk = ATT_BLOCK
    per = nblk * q_blk
    tile = pl.program_id(1)
    cg = pl.program_id(2)

    @pl.when(jnp.logical_and(pl.program_id(0) == 0, jnp.logical_and(tile == 0, cg == 0)))
    def _():
        col = lax.broadcasted_iota(jnp.int32, (q_blk, 2 * q_blk), 1)
        for h in range(HEADS_PER_GROUP):
            base = jnp.broadcast_to(base_ref[h:h + 1, :], (q_blk, 2 * q_blk))
            tab_h = pltpu.roll(base, 0, 1, stride=1, stride_axis=0)
            bias_ref[h] = tab_h
            bias_ref[HEADS_PER_GROUP + h] = jnp.where(col >= q_blk, tab_h, NEG_INF)

    for st in range(sg):
        dst = st * (per + q_blk)
        kf_ref[dst:dst + q_blk, :] = kp_ref[st * q_blk:(st + 1) * q_blk, :]
        kf_ref[dst + q_blk:dst + q_blk + per, :] = k_ref[st * per:(st + 1) * per, :]
        vf_ref[dst:dst + q_blk, :] = vp_ref[st * q_blk:(st + 1) * q_blk, :]
        vf_ref[dst + q_blk:dst + q_blk + per, :] = v_ref[st * per:(st + 1) * per, :]
    lane = lax.broadcasted_iota(jnp.int32, (q_blk, LANES), 1)
    lo = lane < HEAD_DIM
    scale = HEAD_DIM ** -0.5

    def body(u, carry):
        st, n = (0, u) if sg == 1 else (u, 0)
        r0 = pl.multiple_of(u * q_blk, q_blk)
        k0 = pl.multiple_of((u + st) * q_blk, q_blk)
        out_rows = pl.ds(r0, q_blk) if r == 1 else pl.ds(n * (q_blk * r) + cg * sg + st, q_blk, stride=r)
        tab = jnp.where(jnp.logical_and(tile == 0, n == 0), HEADS_PER_GROUP, 0)
        lse_tile = jnp.zeros((q_blk, LANES), F32)
        for pr in range(HEADS_PER_GROUP // 2):
            cs = slice(pr * LANES, (pr + 1) * LANES)
            q2 = q_ref[pl.ds(r0, q_blk), cs]
            k2 = kf_ref[pl.ds(k0, 2 * q_blk), cs]
            v2 = vf_ref[pl.ds(k0, 2 * q_blk), cs]
            outs = []
            for hh in range(2):
                h = 2 * pr + hh
                sel = lo if hh == 0 else jnp.logical_not(lo)
                qm = jnp.where(sel, q2 * scale, jnp.zeros_like(q2))
                s = lax.dot_general(qm, k2, (((1,), (1,)), ((), ())), preferred_element_type=F32)
                s = s + bias_ref[tab + h]
                m = jnp.max(s, axis=-1, keepdims=True)
                p = jnp.exp(s - m)
                den = jnp.sum(p, axis=-1, keepdims=True)
                o = jnp.dot(p.astype(BF16), v2, preferred_element_type=F32) / den
                outs.append(o)
                lse_tile = jnp.where(lane == h, m + jnp.log(den), lse_tile)
            o_ref[pr, out_rows, :] = jnp.where(lo, outs[0], outs[1]).astype(o_ref.dtype)
        lse_ref[out_rows, :] = lse_tile
        return carry

    lax.fori_loop(0, sg * nblk, body, 0, unroll=min(sg * nblk, ATTN_UNROLL))


def _attention_group(qkv, bias, g, dilation, bsz, seq):
    r = dilation
    assert seq % TOKEN_TILE == 0 and TOKEN_TILE % (r * ATT_BLOCK) == 0
    ntile = seq // TOKEN_TILE
    per = TOKEN_TILE // r
    nblk = per // ATT_BLOCK
    sg = min(r, ATTN_UNROLL) if nblk == 1 else 1
    ncg = r // sg
    gpw = QKV_WIDTH // WIDTH_A

    def cur(which):
        return lambda b, t, c: ((b * ntile + t) * ncg + c, which * gpw + g)

    def prev(which):
        return lambda b, t, c: (((b * ntile + jnp.maximum(t - 1, 0)) * ncg + c) * nblk + nblk - 1, which * gpw + g)

    return pl.pallas_call(
        functools.partial(_attn_kernel, nblk=nblk, r=r, sg=sg),
        grid=(bsz, ntile, ncg),
        in_specs=[
            pl.BlockSpec((sg * per, WIDTH_A), cur(0)),
            pl.BlockSpec((sg * per, WIDTH_A), cur(1)),
            pl.BlockSpec((sg * per, WIDTH_A), cur(2)),
            pl.BlockSpec((sg * ATT_BLOCK, WIDTH_A), prev(1)),
            pl.BlockSpec((sg * ATT_BLOCK, WIDTH_A), prev(2)),
            _resident((HEADS_PER_GROUP, 2 * ATT_BLOCK), lambda b, t, c: (0, 0)),
        ],
        out_specs=[
            pl.BlockSpec((WIDTH_A // LANES, TOKEN_TILE, LANES), lambda b, t, c: (0, b * ntile + t, 0)),
            pl.BlockSpec((TOKEN_TILE, LANES), lambda b, t, c: (b * ntile + t, 0)),
        ],
        out_shape=[
            jax.ShapeDtypeStruct((WIDTH_A // LANES, bsz * seq, LANES), F32),
            jax.ShapeDtypeStruct((bsz * seq, LANES), F32),
        ],
        scratch_shapes=[
            pltpu.VMEM((sg * (per + ATT_BLOCK), WIDTH_A), BF16),
            pltpu.VMEM((sg * (per + ATT_BLOCK), WIDTH_A), BF16),
            pltpu.VMEM((2 * HEADS_PER_GROUP, ATT_BLOCK, 2 * ATT_BLOCK), F32),
        ],
        compiler_params=_params(("arbitrary", "arbitrary", "arbitrary")),
        name=f"attn_d{r}",
    )(qkv, qkv, qkv, qkv, qkv, bias)


def _t5_bucket(dist):
    max_exact = N_REL_BUCKETS // 2
    d = np.maximum(dist, 1).astype(np.float32)
    scale = (N_REL_BUCKETS - max_exact) / math.log(REL_MAX_DIST / max_exact)
    large = max_exact + (np.log(d / max_exact) * scale).astype(np.int32)
    large = np.minimum(large, N_REL_BUCKETS - 1)
    return np.where(dist < max_exact, dist, large).astype(np.int32)


def _group_rel_bias(rel_bias, g, dilation, n_steps):
    assert n_steps <= ATT_BLOCK
    dist = (ATT_BLOCK - np.arange(2 * ATT_BLOCK)) % (2 * ATT_BLOCK)
    cols = rel_bias[:, g * HEADS_PER_GROUP:(g + 1) * HEADS_PER_GROUP]
    vals = jnp.transpose(cols[_t5_bucket(dist * dilation)]).astype(F32)
    return jnp.where((dist <= n_steps)[None], vals, NEG_INF)


def _gmlp_kernel(x_ref, wz_ref, bz_ref, g_ref, b_ref, w_ref, bs_ref, o_ref, wzb_ref, wsb_ref, *, nsub):
    @pl.when(pl.program_id(0) == 0)
    def _():
        wzb_ref[...] = wz_ref[...].astype(BF16)
        row = lax.broadcasted_iota(jnp.int32, (CHUNK, CHUNK), 0)
        col = lax.broadcasted_iota(jnp.int32, (CHUNK, CHUNK), 1)
        for g in range(N_GROUPS_B):
            wsb_ref[g] = jnp.where(row >= col, w_ref[g], 0.0).astype(BF16)

    sub = x_ref.shape[0] // nsub
    nchunk = sub // CHUNK
    bs = bs_ref[...]
    for si in range(nsub):
        r0 = si * sub
        z = jnp.dot(x_ref[r0:r0 + sub, :], wzb_ref[...], preferred_element_type=F32) + bz_ref[...]
        z = jax.nn.gelu(z)
        u = z[:, :WIDTH_B]
        v = _layer_norm_rows(z[:, WIDTH_B:], g_ref[...], b_ref[...]).astype(BF16)
        for g in range(N_GROUPS_B):
            cs = slice(g * LANES, (g + 1) * LANES)
            rhs = jnp.concatenate([v[n * CHUNK:(n + 1) * CHUNK, cs] for n in range(nchunk)], axis=1)
            mixed = jnp.dot(wsb_ref[g], rhs, preferred_element_type=F32)
            for n in range(nchunk):
                rs = slice(n * CHUNK, (n + 1) * CHUNK)
                o_ref[r0 + n * CHUNK:r0 + (n + 1) * CHUNK, cs] = (
                    u[rs, cs] * (mixed[:, n * LANES:(n + 1) * LANES] + bs[:, cs])).astype(o_ref.dtype)


def _spatial_gating(xb, w_in, b_in, zb_col, ln_g, ln_b, w_s, b_s, layer, tm, nsub):
    t, d = xb.shape
    zw = 2 * WIDTH_B
    assert zb_col % zw == 0
    bs = jnp.repeat(jnp.transpose(b_s[layer]), LANES, axis=1)
    return pl.pallas_call(
        functools.partial(_gmlp_kernel, nsub=nsub),
        grid=(t // tm,),
        in_specs=[
            pl.BlockSpec((tm, d), lambda i: (i, 0)),
            _resident((None, d, zw), lambda i: (layer, 0, zb_col // zw)),
            _resident((None, 1, zw), lambda i: (layer, 0, zb_col // zw)),
            _resident((None, 1, WIDTH_B), lambda i: (layer, 0, 0)),
            _resident((None, 1, WIDTH_B), lambda i: (layer, 0, 0)),
            _resident((None, N_GROUPS_B, CHUNK, CHUNK), lambda i: (layer, 0, 0, 0)),
            _resident((CHUNK, WIDTH_B), lambda i: (0, 0)),
        ],
        out_specs=pl.BlockSpec((tm, WIDTH_B), lambda i: (i, 0)),
        out_shape=jax.ShapeDtypeStruct((t, WIDTH_B), BF16),
        scratch_shapes=[pltpu.VMEM((d, zw), BF16), pltpu.VMEM((N_GROUPS_B, CHUNK, CHUNK), BF16)],
        compiler_params=_params(("arbitrary",)),
        name="gmlp",
    )(xb, w_in, b_in, ln_g, ln_b, w_s, bs)


S5_CHUNK = 8
CHUNK_LANES = S5_CHUNK * SLAB
STATE_LANES = 2 * SLAB_STATES
N_POWERS = 2 * SUBLANES


def _s5_prep_kernel(lr_ref, li_ref, dt_ref, bre_ref, bim_ref, cre_ref, cim_ref, w_ref, mo_ref, coef_ref):
    nl, ns = S5_CHUNK, SLAB_STATES
    lr = lr_ref[...]
    li = li_ref[...]
    dt = jnp.exp(dt_ref[...])

    def apow(k):
        mag = jnp.exp(lr * dt * k)
        ang = li * dt * k
        return mag * jnp.cos(ang), mag * jnp.sin(ang)

    pr, pi = apow(lax.broadcasted_iota(jnp.int32, (N_POWERS, 1), 0).astype(F32))
    ab_re, ab_im = pr[1:2], pi[1:2]
    nrm = lr * lr + li * li
    zr = ((ab_re - 1.0) * lr + ab_im * li) / nrm
    zi = (ab_im * lr - (ab_re - 1.0) * li) / nrm
    dr = pr * zr - pi * zi
    di = pr * zi + pi * zr
    pr_c, pi_c, dr_c, di_c = pr.T, pi.T, dr.T, di.T
    bre, bim = bre_ref[...], bim_ref[...]
    cre, cim = cre_ref[...], cim_ref[...]
    mo_ref[0:CHUNK_LANES, :] = jnp.zeros((CHUNK_LANES, CHUNK_LANES), mo_ref.dtype)
    for e in range(nl):
        j = nl - 1 - e
        rows = slice(j * SLAB, (j + 1) * SLAB)
        w_ref[rows, 0:ns] = (bre * dr[e:e + 1] - bim * di[e:e + 1]).astype(w_ref.dtype)
        w_ref[rows, ns:2 * ns] = (bre * di[e:e + 1] + bim * dr[e:e + 1]).astype(w_ref.dtype)
        cdr = cre * dr_c[:, e:e + 1] - cim * di_c[:, e:e + 1]
        cdi = cre * di_c[:, e:e + 1] + cim * dr_c[:, e:e + 1]
        kt = (jnp.dot(bre, cdr, preferred_element_type=F32, precision=lax.Precision.HIGHEST)
              - jnp.dot(bim, cdi, preferred_element_type=F32, precision=lax.Precision.HIGHEST)).astype(mo_ref.dtype)
        for jj in range(nl - e):
            mo_ref[jj * SLAB:(jj + 1) * SLAB, (jj + e) * SLAB:(jj + e + 1) * SLAB] = kt
        cols = slice(e * SLAB, (e + 1) * SLAB)
        qr, qi = pr_c[:, e + 1:e + 2], pi_c[:, e + 1:e + 2]
        mo_ref[CHUNK_LANES:CHUNK_LANES + ns, cols] = (cre * qr - cim * qi).astype(mo_ref.dtype)
        mo_ref[CHUNK_LANES + ns:CHUNK_LANES + 2 * ns, cols] = (-(cre * qi + cim * qr)).astype(mo_ref.dtype)
    ar, ai = apow(((lax.broadcasted_iota(jnp.int32, (SUBLANES, 1), 0) + 1) * nl).astype(F32))
    rowi = lax.broadcasted_iota(jnp.int32, (SUBLANES, ns), 0)
    for idx, kk in enumerate((1, 2, 4)):
        coef_ref[2 * idx] = jnp.where(rowi >= kk, ar[kk - 1:kk], 0.0)
        coef_ref[2 * idx + 1] = jnp.where(rowi >= kk, ai[kk - 1:kk], 0.0)
    coef_ref[6] = ar
    coef_ref[7] = ai


def _s5_prep(lam_re, lam_im, log_dt, b_re, b_im, c_re, c_im):
    depth = lam_re.shape[0]
    g8, p, h = GROUPS_PER_SLAB, SSM_STATE, SSM_GROUP
    eye = jnp.eye(g8, dtype=F32)
    row = lambda a: a.astype(F32).reshape(depth, N_SLABS, 1, SLAB_STATES)
    dt = row(jnp.repeat(log_dt.astype(F32)[..., None], p, axis=-1))

    def b_diag(b):
        b = b.astype(F32).reshape(depth, N_SLABS, g8, p, h)
        return jnp.einsum("dsgph,gk->dsghkp", b, eye).reshape(depth, N_SLABS, SLAB, SLAB_STATES)

    def c_diag(c):
        c = c.astype(F32).reshape(depth, N_SLABS, g8, h, p)
        return jnp.einsum("dsghp,gk->dsgpkh", c, eye).reshape(depth, N_SLABS, SLAB_STATES, SLAB)

    blk = lambda a, b: pl.BlockSpec((None, None, a, b), lambda d, s: (d, s, 0, 0))
    return pl.pallas_call(
        _s5_prep_kernel,
        grid=(depth, N_SLABS),
        in_specs=[blk(1, SLAB_STATES)] * 3 + [blk(SLAB, SLAB_STATES)] * 2 + [blk(SLAB_STATES, SLAB)] * 2,
        out_specs=[blk(CHUNK_LANES, STATE_LANES), blk(CHUNK_LANES + STATE_LANES, CHUNK_LANES),
                   pl.BlockSpec((None, None, 8, SUBLANES, SLAB_STATES), lambda d, s: (d, s, 0, 0, 0))],
        out_shape=[jax.ShapeDtypeStruct((depth, N_SLABS, CHUNK_LANES, STATE_LANES), BF16),
                   jax.ShapeDtypeStruct((depth, N_SLABS, CHUNK_LANES + STATE_LANES, CHUNK_LANES), BF16),
                   jax.ShapeDtypeStruct((depth, N_SLABS, 8, SUBLANES, SLAB_STATES), F32)],
        compiler_params=_params(("arbitrary", "arbitrary")),
        name="s5_prep",
    )(row(lam_re), row(lam_im), dt, b_diag(b_re), b_diag(b_im), c_diag(c_re), c_diag(c_im))


def _s5_kernel(u_ref, w_ref, mo_ref, coef_ref, d_ref, o_ref, ubuf_ref):
    for bb in range(u_ref.shape[0]):
        _s5_sequence(bb, u_ref, w_ref, mo_ref, coef_ref, d_ref, o_ref, ubuf_ref)


def _s5_sequence(bb, u_ref, w_ref, mo_ref, coef_ref, d_ref, o_ref, ubuf_ref):
    nl = S5_CHUNK
    seq = u_ref.shape[1]
    n = seq // nl
    nv = n // SUBLANES
    ns = SLAB_STATES
    ubuf_ref[bb, 0:nl, :] = jnp.zeros((nl, SLAB), F32)
    ubuf_ref[bb, nl:, :] = u_ref[bb]
    cur = [ubuf_ref[bb, pl.ds(nl + j, n, stride=nl), :] for j in range(nl)]
    prev = [ubuf_ref[bb, pl.ds(j, n, stride=nl), :] for j in range(nl)]
    u2 = jnp.concatenate([t.astype(BF16) for t in cur], axis=1)
    u2p = jnp.concatenate([t.astype(BF16) for t in prev], axis=1)
    sc = jnp.dot(u2p, w_ref[...], preferred_element_type=F32)
    xr = sc[:, :ns].reshape(nv, SUBLANES, ns)
    xi = sc[:, ns:].reshape(nv, SUBLANES, ns)
    for idx, k in enumerate((1, 2, 4)):
        cr = coef_ref[2 * idx]
        ci = coef_ref[2 * idx + 1]
        sr = pltpu.roll(xr, k, 1)
        si = pltpu.roll(xi, k, 1)
        xr, xi = xr + (cr * sr - ci * si), xi + (cr * si + ci * sr)
    pr = coef_ref[6]
    pi = coef_ref[7]
    c_r = jnp.zeros((1, ns), F32)
    c_i = jnp.zeros((1, ns), F32)
    out_r, out_i = [], []
    for r in range(nv):
        yr = xr[r] + (pr * c_r - pi * c_i)
        yi = xi[r] + (pr * c_i + pi * c_r)
        out_r.append(yr)
        out_i.append(yi)
        c_r = yr[SUBLANES - 1:SUBLANES, :]
        c_i = yi[SUBLANES - 1:SUBLANES, :]
    x = jnp.concatenate([jnp.concatenate(out_r, axis=0), jnp.concatenate(out_i, axis=0)], axis=1)
    lhs = jnp.concatenate([u2, x.astype(BF16)], axis=1)
    y = jnp.dot(lhs, mo_ref[...], preferred_element_type=F32)
    d = d_ref[...]
    for j in range(nl):
        o_ref[bb, pl.ds(j, n, stride=nl), :] = jax.nn.gelu(y[:, j * SLAB:(j + 1) * SLAB] + d * cur[j]).astype(o_ref.dtype)


def _s5(zu, uc_col, wmat, mo, coef, d_skip, layer, bsz, seq, nb):
    assert seq % (S5_CHUNK * SUBLANES) == 0 and bsz % nb == 0
    zu3 = zu.reshape(bsz, seq, zu.shape[-1])
    c0 = uc_col // SLAB
    out = pl.pallas_call(
        _s5_kernel,
        grid=(N_SLABS, bsz // nb),
        in_specs=[
            pl.BlockSpec((nb, seq, SLAB), lambda s, b: (b, 0, c0 + s)),
            pl.BlockSpec((None, None, CHUNK_LANES, STATE_LANES), lambda s, b: (layer, s, 0, 0)),
            pl.BlockSpec((None, None, CHUNK_LANES + STATE_LANES, CHUNK_LANES), lambda s, b: (layer, s, 0, 0)),
            pl.BlockSpec((None, None, 8, SUBLANES, SLAB_STATES), lambda s, b: (layer, s, 0, 0, 0)),
            pl.BlockSpec((None, 1, SLAB), lambda s, b: (layer, 0, s)),
        ],
        out_specs=pl.BlockSpec((nb, seq, SLAB), lambda s, b: (b, 0, s)),
        out_shape=jax.ShapeDtypeStruct((bsz, seq, WIDTH_C), F32),
        scratch_shapes=[pltpu.VMEM((nb, seq + S5_CHUNK, SLAB), F32)],
        compiler_params=_params(("arbitrary", "arbitrary")),
        name="s5_scan",
    )(zu3, wmat, mo, coef, d_skip)
    return out.reshape(bsz * seq, WIDTH_C)


def _merge_kernel(o0_ref, o1_ref, o2_ref, l0_ref, l1_ref, l2_ref, yb_ref, yc_ref,
                  g0_ref, g1_ref, g2_ref, e_ref, wpa_ref, wpb_ref, wpc_ref, wglu_ref, bglu_ref, out_ref):
    l0, l1, l2 = l0_ref[...], l1_ref[...], l2_ref[...]
    m = jnp.maximum(jnp.maximum(l0, l1), l2)
    e0, e1, e2 = jnp.exp(l0 - m), jnp.exp(l1 - m), jnp.exp(l2 - m)
    den = e0 + e1 + e2
    e = e_ref[...]

    def expand(w):
        hi = w.astype(BF16)
        mid = (w - hi.astype(F32)).astype(BF16)
        return jnp.dot(jnp.concatenate([hi, mid], axis=1), e, preferred_element_type=F32)

    planes = lambda o_ref: jnp.concatenate([o_ref[p] for p in range(o_ref.shape[0])], axis=1)
    ya = (expand(e0 / den) * planes(o0_ref) + expand(e1 / den) * planes(o1_ref) + expand(e2 / den) * planes(o2_ref))
    pa = jnp.dot(ya.astype(BF16), wpa_ref[...], preferred_element_type=F32)
    pb = jnp.dot(yb_ref[...], wpb_ref[...], preferred_element_type=F32)
    yc = yc_ref[...]
    glu = yc * jax.nn.sigmoid(jnp.dot(yc.astype(BF16), wglu_ref[...], preferred_element_type=F32) + bglu_ref[...])
    pc = jnp.dot(glu.astype(BF16), wpc_ref[...], preferred_element_type=F32)
    merged = (g0_ref[...].astype(F32) * pa + g1_ref[...].astype(F32) * pb + g2_ref[...].astype(F32) * pc)
    out_ref[...] = merged.astype(out_ref.dtype)


def _resident(shape, index_map):
    return pl.BlockSpec(shape, index_map, pipeline_mode=pl.Buffered(1))


def _merge(outs, lses, yb, yc, gates, expand, w_pa, w_pb, w_pc, w_glu, b_glu, layer, d_model, tm):
    t = yb.shape[0]
    row = lambda w: pl.BlockSpec((tm, w), lambda i: (i, 0))
    full = lambda a, b: _resident((None, a, b), lambda i: (layer, 0, 0))
    return pl.pallas_call(
        _merge_kernel,
        grid=(t // tm,),
        in_specs=[pl.BlockSpec((WIDTH_A // LANES, tm, LANES), lambda i: (0, i, 0))] * 3
        + [row(LANES)] * 3 + [row(WIDTH_B), row(WIDTH_C)]
        + [pl.BlockSpec((tm, d_model), lambda i, k=k: (i, k)) for k in range(N_BRANCH)]
        + [_resident((2 * LANES, WIDTH_A), lambda i: (0, 0)),
           full(WIDTH_A, d_model), full(WIDTH_B, d_model), full(WIDTH_C, d_model),
           full(WIDTH_C, WIDTH_C), full(1, WIDTH_C)],
        out_specs=pl.BlockSpec((tm, d_model), lambda i: (i, 0)),
        out_shape=jax.ShapeDtypeStruct((t, d_model), BF16),
        compiler_params=_params(("arbitrary",)),
        name="merge",
    )(*outs, *lses, yb, yc, gates, gates, gates, expand, w_pa, w_pb, w_pc, w_glu, b_glu)


def _proj_ln_kernel(m_ref, x_ref, w_ref, g_ref, b_ref, o_ref, ob_ref, *, alpha):
    half = m_ref.shape[0] // 2
    for rs in (slice(0, half), slice(half, 2 * half)):
        f = jnp.dot(m_ref[rs, :], w_ref[...], preferred_element_type=F32)
        y = _layer_norm_rows(alpha * x_ref[rs, :] + f, g_ref[...], b_ref[...])
        o_ref[rs, :] = y
        ob_ref[rs, :] = y.astype(BF16)


def _proj_ln(merged, x, w_o, ln_g, ln_b, layer, alpha, tm):
    t, d = x.shape
    return pl.pallas_call(
        functools.partial(_proj_ln_kernel, alpha=alpha),
        grid=(t // tm,),
        in_specs=[
            pl.BlockSpec((tm, d), lambda i: (i, 0)),
            pl.BlockSpec((tm, d), lambda i: (i, 0)),
            _resident((None, d, d), lambda i: (layer, 0, 0)),
            _resident((None, 1, d), lambda i: (layer, 0, 0)),
            _resident((None, 1, d), lambda i: (layer, 0, 0)),
        ],
        out_specs=[pl.BlockSpec((tm, d), lambda i: (i, 0)), pl.BlockSpec((tm, d), lambda i: (i, 0))],
        out_shape=[jax.ShapeDtypeStruct((t, d), F32), jax.ShapeDtypeStruct((t, d), BF16)],
        compiler_params=_params(("arbitrary",)),
        name="out_proj_ln",
    )(merged, x, w_o, ln_g, ln_b)


def _swiglu_kernel(x_ref, wg_ref, wu_ref, o_ref, wgb_ref, wub_ref):
    @pl.when(pl.program_id(1) == 0)
    def _():
        wgb_ref[...] = wg_ref[...].astype(BF16)
        wub_ref[...] = wu_ref[...].astype(BF16)

    x = x_ref[...]
    gate = jnp.dot(x, wgb_ref[...], preferred_element_type=F32)
    up = jnp.dot(x, wub_ref[...], preferred_element_type=F32)
    o_ref[...] = (jax.nn.silu(gate) * up).astype(o_ref.dtype)


def _swiglu(xb, w_in, layer, d_ff, bm, bn):
    t, d = xb.shape
    nj = d_ff // bn
    return pl.pallas_call(
        _swiglu_kernel,
        grid=(nj, t // bm),
        in_specs=[
            pl.BlockSpec((bm, d), lambda j, i: (i, 0)),
            pl.BlockSpec((None, d, bn), lambda j, i: (layer, 0, j)),
            pl.BlockSpec((None, d, bn), lambda j, i: (layer, 0, j + nj)),
        ],
        out_specs=pl.BlockSpec((bm, bn), lambda j, i: (i, j)),
        out_shape=jax.ShapeDtypeStruct((t, d_ff), BF16),
        scratch_shapes=[pltpu.VMEM((d, bn), BF16), pltpu.VMEM((d, bn), BF16)],
        compiler_params=_params(("arbitrary", "arbitrary"), VMEM_LIMIT_LARGE),
        name="ffn_in",
    )(xb, w_in, w_in)


def _ffn_out_ln_kernel(h_ref, w_ref, x_ref, g_ref, b_ref, o_ref, ob_ref, *, alpha, nk):
    k = pl.program_id(1)

    @pl.when(k == 0)
    def _():
        o_ref[...] = jnp.dot(h_ref[...], w_ref[...], preferred_element_type=F32)

    @pl.when(jnp.logical_and(k > 0, k < nk - 1))
    def _():
        o_ref[...] += jnp.dot(h_ref[...], w_ref[...], preferred_element_type=F32)

    @pl.when(k == nk - 1)
    def _():
        half = h_ref.shape[0] // 2
        for rs in (slice(0, half), slice(half, 2 * half)):
            part = jnp.dot(h_ref[rs, :], w_ref[...], preferred_element_type=F32)
            y = _layer_norm_rows(alpha * x_ref[rs, :] + (o_ref[rs, :] + part), g_ref[...], b_ref[...])
            o_ref[rs, :] = y
            ob_ref[rs, :] = y.astype(BF16)


def _ffn_out_ln(h, x, w_out, ln_g, ln_b, layer, alpha, bm, bk):
    t, d = x.shape
    d_ff = h.shape[1]
    nk = d_ff // bk
    return pl.pallas_call(
        functools.partial(_ffn_out_ln_kernel, alpha=alpha, nk=nk),
        grid=(t // bm, nk),
        in_specs=[
            pl.BlockSpec((bm, bk), lambda i, k: (i, k)),
            pl.BlockSpec((None, bk, d), lambda i, k: (layer, k, 0)),
            pl.BlockSpec((bm, d), lambda i, k: (i, 0)),
            pl.BlockSpec((None, 1, d), lambda i, k: (layer, 0, 0)),
            pl.BlockSpec((None, 1, d), lambda i, k: (layer, 0, 0)),
        ],
        out_specs=[pl.BlockSpec((bm, d), lambda i, k: (i, 0)), pl.BlockSpec((bm, d), lambda i, k: (i, 0))],
        out_shape=[jax.ShapeDtypeStruct((t, d), F32), jax.ShapeDtypeStruct((t, d), BF16)],
        compiler_params=_params(("arbitrary", "arbitrary"), VMEM_LIMIT_LARGE),
        name="ffn_out_ln",
    )(h, w_out, x, ln_g, ln_b)


@jax.jit
def kernel(x, w_in, b_in, rel_bias, sgu_ln_g, sgu_ln_b, w_s, b_s, lam_re, lam_im, log_dt, b_re, b_im, c_re, c_im, d_skip, w_glu, b_glu, w_pa, w_pb, w_pc, w_o, ln1_g, ln1_b, w_ffn_in, w_ffn_out, ln2_g, ln2_b):
    bsz, seq, d_model = x.shape
    depth = w_in.shape[0]
    d_ff = w_ffn_out.shape[1]
    t = bsz * seq
    alpha = (2 * depth) ** 0.25
    in_cols = w_in.shape[-1]
    gl_width = N_BRANCH * d_model
    zb_src = 3 * QKV_WIDTH
    uc_src = zb_src + 2 * WIDTH_B
    gl_src = uc_src + WIDTH_C
    assert gl_src + gl_width == in_cols
    bn = 768

    bf = lambda a: a.astype(BF16)
    row3 = lambda a: a.reshape(a.shape[0], 1, a.shape[1]).astype(F32)
    w_pa_b, w_pb_b, w_pc_b, w_glu_b = bf(w_pa), bf(w_pb), bf(w_pc), bf(w_glu)
    w_o_b, w_ffn_out_b = bf(w_o), bf(w_ffn_out)
    w_in32, w_ffn_in32 = w_in.astype(F32), w_ffn_in.astype(F32)
    b_in3, b_glu3, d_skip3 = row3(b_in), row3(b_glu), row3(d_skip)
    sgu_g3, sgu_b3 = row3(sgu_ln_g), row3(sgu_ln_b)
    ln1_g3, ln1_b3, ln2_g3, ln2_b3 = row3(ln1_g), row3(ln1_b), row3(ln2_g), row3(ln2_b)
    w_s32 = w_s.astype(F32)

    group_bias = [_group_rel_bias(rel_bias, g, dil, win // dil) for g, (win, dil) in enumerate(ATT_PATTERNS)]
    expand = jnp.asarray(np.tile(np.kron(np.eye(LANES, HEADS_PER_GROUP, dtype=np.float32),
                                         np.ones((1, HEAD_DIM), np.float32)), (2, 1)), dtype=BF16)

    s5_w, s5_mo, s5_coef = _s5_prep(lam_re, lam_im, log_dt, b_re, b_im, c_re, c_im)

    xf = x.reshape(t, d_model).astype(F32)
    xb = bf(xf)
    for l in range(depth):
        qkv = _in_proj(xb, w_in32, b_in3, l, lambda j: j, 3 * QKV_WIDTH, BF16, TOKEN_TILE, WIDTH_A, "streams")
        uc = _in_proj(xb, w_in32, b_in3, l, lambda j: j + uc_src // bn, WIDTH_C, F32, TOKEN_TILE // 2, bn, "plain")
        gates = _in_proj(xb, w_in32, b_in3, l, lambda j: j + gl_src // bn, gl_width, BF16, TOKEN_TILE, bn,
                         "sigmoid")
        outs, lses = [], []
        for g, (window, dil) in enumerate(ATT_PATTERNS):
            o_g, lse_g = _attention_group(qkv, group_bias[g], g, dil, bsz, seq)
            outs.append(o_g)
            lses.append(lse_g)
        yb = _spatial_gating(xb, w_in32, b_in3, zb_src, sgu_g3, sgu_b3, w_s32, b_s, l, 1024, 2)
        yc = _s5(uc, 0, s5_w, s5_mo, s5_coef, d_skip3, l, bsz, seq, 1)
        merged = _merge(outs, lses, yb, yc, gates, expand, w_pa_b, w_pb_b, w_pc_b, w_glu_b, b_glu3,
                        l, d_model, 512)
        xf, xb = _proj_ln(merged, xf, w_o_b, ln1_g3, ln1_b3, l, alpha, 512)
        h = _swiglu(xb, w_ffn_in32, l, d_ff, 2048, 512)
        xf, xb = _ffn_out_ln(h, xf, w_ffn_out_b, ln2_g3, ln2_b3, l, alpha, 512, d_ff // 2)
    return xf.reshape(bsz, seq, d_model).astype(x.dtype)
```

```python
import functools
import math

import numpy as np
import jax
import jax.numpy as jnp
from jax import lax
from jax.experimental import pallas as pl
from jax.experimental.pallas import tpu as pltpu

F32 = jnp.float32
BF16 = jnp.bfloat16

ATT_PATTERNS = ((128, 1), (512, 4), (2048, 16))
HEADS_PER_GROUP = 8
HEAD_DIM = 64
N_HEADS = len(ATT_PATTERNS) * HEADS_PER_GROUP
QKV_WIDTH = N_HEADS * HEAD_DIM
WIDTH_A = HEADS_PER_GROUP * HEAD_DIM
ATT_BLOCK = 128
N_REL_BUCKETS = 32
REL_MAX_DIST = 2048
NEG_INF = -1e30
CHUNK = 128
WIDTH_B = 768
N_GROUPS_B = 6
WIDTH_C = 768
SSM_GROUP = 16
N_GROUPS_C = WIDTH_C // SSM_GROUP
SSM_STATE = 64
N_BRANCH = 3
LN_EPS = 1e-5

LANES = 128
SUBLANES = 8
VMEM_LIMIT = 48 * 1024 * 1024
VMEM_LIMIT_LARGE = 56 * 1024 * 1024

SLAB = LANES
GROUPS_PER_SLAB = SLAB // SSM_GROUP
N_SLABS = WIDTH_C // SLAB
SLAB_STATES = GROUPS_PER_SLAB * SSM_STATE


def _params(sem, vmem=VMEM_LIMIT):
    return pltpu.CompilerParams(dimension_semantics=sem, vmem_limit_bytes=vmem)


def _layer_norm_rows(y, g, b):
    mu = jnp.mean(y, axis=-1, keepdims=True)
    yc = y - mu
    var = jnp.mean(yc * yc, axis=-1, keepdims=True)
    return yc * lax.rsqrt(var + LN_EPS) * g + b


TOKEN_TILE = 2048
ATTN_UNROLL = 8


def _in_proj_kernel(x_ref, w_ref, b_ref, o_ref, wb_ref, *scratch, mode):
    @pl.when(pl.program_id(1) == 0)
    def _():
        wb_ref[...] = w_ref[...].astype(BF16)

    rows = o_ref.shape[0]
    half = rows // 2
    if mode != "streams":
        for rs in (slice(0, half), slice(half, rows)):
            acc = jnp.dot(x_ref[rs, :], wb_ref[...], preferred_element_type=F32) + b_ref[...]
            if mode == "plain":
                o_ref[rs, :] = acc.astype(o_ref.dtype)
            else:
                o_ref[rs, :] = (0.5 * jnp.tanh(0.5 * acc) + 0.5).astype(o_ref.dtype)
        return
    (scr_ref,) = scratch
    grp = pl.program_id(0) % len(ATT_PATTERNS)
    for gi, (_, r) in enumerate(ATT_PATTERNS):
        @pl.when(grp == gi)
        def _(r=r):
            for hf in range(2):
                acc = jnp.dot(x_ref[hf * half:(hf + 1) * half, :], wb_ref[...],
                              preferred_element_type=F32) + b_ref[...]
                if r == 1:
                    o_ref[hf * half:(hf + 1) * half, :] = acc.astype(o_ref.dtype)
                    continue
                per = rows // r
                seg = half // r
                for kt in range(o_ref.shape[1] // LANES):
                    cs = slice(kt * LANES, (kt + 1) * LANES)
                    scr_ref[hf, kt] = acc[:, cs]
                    for c in range(r):
                        o_ref[c * per + hf * seg:c * per + (hf + 1) * seg, cs] = (
                            scr_ref[hf, kt, pl.ds(c, seg, stride=r), :].astype(o_ref.dtype))


def _in_proj(x, w, b, layer, col_map, n_out, out_dtype, bm, bn, mode):
    m, k = x.shape
    scratch = [pltpu.VMEM((k, bn), BF16)]
    if mode == "streams":
        scratch.append(pltpu.VMEM((2, bn // LANES, bm // 2, LANES), F32))
    return pl.pallas_call(
        functools.partial(_in_proj_kernel, mode=mode),
        grid=(n_out // bn, m // bm),
        in_specs=[
            pl.BlockSpec((bm, k), lambda j, i: (i, 0)),
            pl.BlockSpec((None, k, bn), lambda j, i: (layer, 0, col_map(j))),
            pl.BlockSpec((None, 1, bn), lambda j, i: (layer, 0, col_map(j))),
        ],
        out_specs=pl.BlockSpec((bm, bn), lambda j, i: (i, j)),
        out_shape=jax.ShapeDtypeStruct((m, n_out), out_dtype),
        scratch_shapes=scratch,
        compiler_params=_params(("arbitrary", "arbitrary"), VMEM_LIMIT_LARGE),
        name="in_proj_" + mode,
    )(x, w, b)


def _attn_kernel(q_ref, k_ref, v_ref, kp_ref, vp_ref, base_ref, o_ref, lse_ref, kf_ref, vf_ref, bias_ref,
                 *, nblk, r, sg):
    q_blk = ATT_BLOCK
    per = nblk * q_blk
    tile = pl.program_id(1)
    cg = pl.program_id(2)

    @pl.when(jnp.logical_and(pl.program_id(0) == 0, jnp.logical_and(tile == 0, cg == 0)))
    def _():
        col = lax.broadcasted_iota(jnp.int32, (q_blk, 2 * q_blk), 1)
        for h in range(HEADS_PER_GROUP):
            base = jnp.broadcast_to(base_ref[h:h + 1, :], (q_blk, 2 * q_blk))
            tab_h = pltpu.roll(base, 0, 1, stride=1, stride_axis=0)
            bias_ref[h] = tab_h
            bias_ref[HEADS_PER_GROUP + h] = jnp.where(col >= q_blk, tab_h, NEG_INF)

    for st in range(sg):
        dst = st * (per + q_blk)
        kf_ref[dst:dst + q_blk, :] = kp_ref[st * q_blk:(st + 1) * q_blk, :]
        kf_ref[dst + q_blk:dst + q_blk + per, :] = k_ref[st * per:(st + 1) * per, :]
        vf_ref[dst:dst + q_blk, :] = vp_ref[st * q_blk:(st + 1) * q_blk, :]
        vf_ref[dst + q_blk:dst + q_blk + per, :] = v_ref[st * per:(st + 1) * per, :]
    lane = lax.broadcasted_iota(jnp.int32, (q_blk, LANES), 1)
    lo = lane < HEAD_DIM
    scale = HEAD_DIM ** -0.5

    def body(u, carry):
        st, n = (0, u) if sg == 1 else (u, 0)
        r0 = pl.multiple_of(u * q_blk, q_blk)
        k0 = pl.multiple_of((u + st) * q_blk, q_blk)
        out_rows = pl.ds(r0, q_blk) if r == 1 else pl.ds(n * (q_blk * r) + cg * sg + st, q_blk, stride=r)
        tab = jnp.where(jnp.logical_and(tile == 0, n == 0), HEADS_PER_GROUP, 0)
        lse_tile = jnp.zeros((q_blk, LANES), F32)
        for pr in range(HEADS_PER_GROUP // 2):
            cs = slice(pr * LANES, (pr + 1) * LANES)
            q2 = q_ref[pl.ds(r0, q_blk), cs]
            k2 = kf_ref[pl.ds(k0, 2 * q_blk), cs]
            v2 = vf_ref[pl.ds(k0, 2 * q_blk), cs]
            outs = []
            for hh in range(2):
                h = 2 * pr + hh
                sel = lo if hh == 0 else jnp.logical_not(lo)
                qm = jnp.where(sel, q2 * scale, jnp.zeros_like(q2))
                s = lax.dot_general(qm, k2, (((1,), (1,)), ((), ())), preferred_element_type=F32)
                s = s + bias_ref[tab + h]
                m = jnp.max(s, axis=-1, keepdims=True)
                p = jnp.exp(s - m)
                den = jnp.sum(p, axis=-1, keepdims=True)
                o = jnp.dot(p.astype(BF16), v2, preferred_element_type=F32) / den
                outs.append(o)
                lse_tile = jnp.where(lane == h, m + jnp.log(den), lse_tile)
            o_ref[pr, out_rows, :] = jnp.where(lo, outs[0], outs[1]).astype(o_ref.dtype)
        lse_ref[out_rows, :] = lse_tile
        return carry

    lax.fori_loop(0, sg * nblk, body, 0, unroll=min(sg * nblk, ATTN_UNROLL))


def _attention_group(qkv, bias, g, dilation, bsz, seq):
    r = dilation
    assert seq % TOKEN_TILE == 0 and TOKEN_TILE % (r * ATT_BLOCK) == 0
    ntile = seq // TOKEN_TILE
    per = TOKEN_TILE // r
    nblk = per // ATT_BLOCK
    sg = min(r, ATTN_UNROLL) if nblk == 1 else 1
    ncg = r // sg
    gpw = QKV_WIDTH // WIDTH_A

    def cur(which):
        return lambda b, t, c: ((b * ntile + t) * ncg + c, which * gpw + g)

    def prev(which):
        return lambda b, t, c: (((b * ntile + jnp.maximum(t - 1, 0)) * ncg + c) * nblk + nblk - 1, which * gpw + g)

    return pl.pallas_call(
        functools.partial(_attn_kernel, nblk=nblk, r=r, sg=sg),
        grid=(bsz, ntile, ncg),
        in_specs=[
            pl.BlockSpec((sg * per, WIDTH_A), cur(0)),
            pl.BlockSpec((sg * per, WIDTH_A), cur(1)),
            pl.BlockSpec((sg * per, WIDTH_A), cur(2)),
            pl.BlockSpec((sg * ATT_BLOCK, WIDTH_A), prev(1)),
            pl.BlockSpec((sg * ATT_BLOCK, WIDTH_A), prev(2)),
            _resident((HEADS_PER_GROUP, 2 * ATT_BLOCK), lambda b, t, c: (0, 0)),
        ],
        out_specs=[
            pl.BlockSpec((WIDTH_A // LANES, TOKEN_TILE, LANES), lambda b, t, c: (0, b * ntile + t, 0)),
            pl.BlockSpec((TOKEN_TILE, LANES), lambda b, t, c: (b * ntile + t, 0)),
        ],
        out_shape=[
            jax.ShapeDtypeStruct((WIDTH_A // LANES, bsz * seq, LANES), F32),
            jax.ShapeDtypeStruct((bsz * seq, LANES), F32),
        ],
        scratch_shapes=[
            pltpu.VMEM((sg * (per + ATT_BLOCK), WIDTH_A), BF16),
            pltpu.VMEM((sg * (per + ATT_BLOCK), WIDTH_A), BF16),
            pltpu.VMEM((2 * HEADS_PER_GROUP, ATT_BLOCK, 2 * ATT_BLOCK), F32),
        ],
        compiler_params=_params(("arbitrary", "arbitrary", "arbitrary")),
        name=f"attn_d{r}",
    )(qkv, qkv, qkv, qkv, qkv, bias)


def _t5_bucket(dist):
    max_exact = N_REL_BUCKETS // 2
    d = np.maximum(dist, 1).astype(np.float32)
    scale = (N_REL_BUCKETS - max_exact) / math.log(REL_MAX_DIST / max_exact)
    large = max_exact + (np.log(d / max_exact) * scale).astype(np.int32)
    large = np.minimum(large, N_REL_BUCKETS - 1)
    return np.where(dist < max_exact, dist, large).astype(np.int32)


def _group_rel_bias(rel_bias, g, dilation, n_steps):
    assert n_steps <= ATT_BLOCK
    dist = (ATT_BLOCK - np.arange(2 * ATT_BLOCK)) % (2 * ATT_BLOCK)
    cols = rel_bias[:, g * HEADS_PER_GROUP:(g + 1) * HEADS_PER_GROUP]
    vals = jnp.transpose(cols[_t5_bucket(dist * dilation)]).astype(F32)
    return jnp.where((dist <= n_steps)[None], vals, NEG_INF)


def _gmlp_kernel(x_ref, wz_ref, bz_ref, g_ref, b_ref, w_ref, bs_ref, o_ref, wzb_ref, wsb_ref, *, nsub):
    @pl.when(pl.program_id(0) == 0)
    def _():
        wzb_ref[...] = wz_ref[...].astype(BF16)
        row = lax.broadcasted_iota(jnp.int32, (CHUNK, CHUNK), 0)
        col = lax.broadcasted_iota(jnp.int32, (CHUNK, CHUNK), 1)
        for g in range(N_GROUPS_B):
            wsb_ref[g] = jnp.where(row >= col, w_ref[g], 0.0).astype(BF16)

    sub = x_ref.shape[0] // nsub
    nchunk = sub // CHUNK
    bs = bs_ref[...]
    for si in range(nsub):
        r0 = si * sub
        z = jnp.dot(x_ref[r0:r0 + sub, :], wzb_ref[...], preferred_element_type=F32) + bz_ref[...]
        z = jax.nn.gelu(z)
        u = z[:, :WIDTH_B]
        v = _layer_norm_rows(z[:, WIDTH_B:], g_ref[...], b_ref[...]).astype(BF16)
        for g in range(N_GROUPS_B):
            cs = slice(g * LANES, (g + 1) * LANES)
            rhs = jnp.concatenate([v[n * CHUNK:(n + 1) * CHUNK, cs] for n in range(nchunk)], axis=1)
            mixed = jnp.dot(wsb_ref[g], rhs, preferred_element_type=F32)
            for n in range(nchunk):
                rs = slice(n * CHUNK, (n + 1) * CHUNK)
                o_ref[r0 + n * CHUNK:r0 + (n + 1) * CHUNK, cs] = (
                    u[rs, cs] * (mixed[:, n * LANES:(n + 1) * LANES] + bs[:, cs])).astype(o_ref.dtype)


def _spatial_gating(xb, w_in, b_in, zb_col, ln_g, ln_b, w_s, b_s, layer, tm, nsub):
    t, d = xb.shape
    zw = 2 * WIDTH_B
    assert zb_col % zw == 0
    bs = jnp.repeat(jnp.transpose(b_s[layer]), LANES, axis=1)
    return pl.pallas_call(
        functools.partial(_gmlp_kernel, nsub=nsub),
        grid=(t // tm,),
        in_specs=[
            pl.BlockSpec((tm, d), lambda i: (i, 0)),
            _resident((None, d, zw), lambda i: (layer, 0, zb_col // zw)),
            _resident((None, 1, zw), lambda i: (layer, 0, zb_col // zw)),
            _resident((None, 1, WIDTH_B), lambda i: (layer, 0, 0)),
            _resident((None, 1, WIDTH_B), lambda i: (layer, 0, 0)),
            _resident((None, N_GROUPS_B, CHUNK, CHUNK), lambda i: (layer, 0, 0, 0)),
            _resident((CHUNK, WIDTH_B), lambda i: (0, 0)),
        ],
        out_specs=pl.BlockSpec((tm, WIDTH_B), lambda i: (i, 0)),
        out_shape=jax.ShapeDtypeStruct((t, WIDTH_B), BF16),
        scratch_shapes=[pltpu.VMEM((d, zw), BF16), pltpu.VMEM((N_GROUPS_B, CHUNK, CHUNK), BF16)],
        compiler_params=_params(("arbitrary",)),
        name="gmlp",
    )(xb, w_in, b_in, ln_g, ln_b, w_s, bs)


S5_CHUNK = 8
CHUNK_LANES = S5_CHUNK * SLAB
STATE_LANES = 2 * SLAB_STATES
N_POWERS = 2 * SUBLANES


def _s5_prep_kernel(lr_ref, li_ref, dt_ref, bre_ref, bim_ref, cre_ref, cim_ref, w_ref, mo_ref, coef_ref):
    nl, ns = S5_CHUNK, SLAB_STATES
    lr = lr_ref[...]
    li = li_ref[...]
    dt = jnp.exp(dt_ref[...])

    def apow(k):
        mag = jnp.exp(lr * dt * k)
        ang = li * dt * k
        return mag * jnp.cos(ang), mag * jnp.sin(ang)

    pr, pi = apow(lax.broadcasted_iota(jnp.int32, (N_POWERS, 1), 0).astype(F32))
    ab_re, ab_im = pr[1:2], pi[1:2]
    nrm = lr * lr + li * li
    zr = ((ab_re - 1.0) * lr + ab_im * li) / nrm
    zi = (ab_im * lr - (ab_re - 1.0) * li) / nrm
    dr = pr * zr - pi * zi
    di = pr * zi + pi * zr
    pr_c, pi_c, dr_c, di_c = pr.T, pi.T, dr.T, di.T
    bre, bim = bre_ref[...], bim_ref[...]
    cre, cim = cre_ref[...], cim_ref[...]
    mo_ref[0:CHUNK_LANES, :] = jnp.zeros((CHUNK_LANES, CHUNK_LANES), mo_ref.dtype)
    for e in range(nl):
        j = nl - 1 - e
        rows = slice(j * SLAB, (j + 1) * SLAB)
        w_ref[rows, 0:ns] = (bre * dr[e:e + 1] - bim * di[e:e + 1]).astype(w_ref.dtype)
        w_ref[rows, ns:2 * ns] = (bre * di[e:e + 1] + bim * dr[e:e + 1]).astype(w_ref.dtype)
        cdr = cre * dr_c[:, e:e + 1] - cim * di_c[:, e:e + 1]
        cdi = cre * di_c[:, e:e + 1] + cim * dr_c[:, e:e + 1]
        kt = (jnp.dot(bre, cdr, preferred_element_type=F32, precision=lax.Precision.HIGHEST)
              - jnp.dot(bim, cdi, preferred_element_type=F32, precision=lax.Precision.HIGHEST)).astype(mo_ref.dtype)
        for jj in range(nl - e):
            mo_ref[jj * SLAB:(jj + 1) * SLAB, (jj + e) * SLAB:(jj + e + 1) * SLAB] = kt
        cols = slice(e * SLAB, (e + 1) * SLAB)
        qr, qi = pr_c[:, e + 1:e + 2], pi_c[:, e + 1:e + 2]
        mo_ref[CHUNK_LANES:CHUNK_LANES + ns, cols] = (cre * qr - cim * qi).astype(mo_ref.dtype)
        mo_ref[CHUNK_LANES + ns:CHUNK_LANES + 2 * ns, cols] = (-(cre * qi + cim * qr)).astype(mo_ref.dtype)
    ar, ai = apow(((lax.broadcasted_iota(jnp.int32, (SUBLANES, 1), 0) + 1) * nl).astype(F32))
    rowi = lax.broadcasted_iota(jnp.int32, (SUBLANES, ns), 0)
    for idx, kk in enumerate((1, 2, 4)):
        coef_ref[2 * idx] = jnp.where(rowi >= kk, ar[kk - 1:kk], 0.0)
        coef_ref[2 * idx + 1] = jnp.where(rowi >= kk, ai[kk - 1:kk], 0.0)
    coef_ref[6] = ar
    coef_ref[7] = ai


def _s5_prep(lam_re, lam_im, log_dt, b_re, b_im, c_re, c_im):
    depth = lam_re.shape[0]
    g8, p, h = GROUPS_PER_SLAB, SSM_STATE, SSM_GROUP
    eye = jnp.eye(g8, dtype=F32)
    row = lambda a: a.astype(F32).reshape(depth, N_SLABS, 1, SLAB_STATES)
    dt = row(jnp.repeat(log_dt.astype(F32)[..., None], p, axis=-1))

    def b_diag(b):
        b = b.astype(F32).reshape(depth, N_SLABS, g8, p, h)
        return jnp.einsum("dsgph,gk->dsghkp", b, eye).reshape(depth, N_SLABS, SLAB, SLAB_STATES)

    def c_diag(c):
        c = c.astype(F32).reshape(depth, N_SLABS, g8, h, p)
        return jnp.einsum("dsghp,gk->dsgpkh", c, eye).reshape(depth, N_SLABS, SLAB_STATES, SLAB)

    blk = lambda a, b: pl.BlockSpec((None, None, a, b), lambda d, s: (d, s, 0, 0))
    return pl.pallas_call(
        _s5_prep_kernel,
        grid=(depth, N_SLABS),
        in_specs=[blk(1, SLAB_STATES)] * 3 + [blk(SLAB, SLAB_STATES)] * 2 + [blk(SLAB_STATES, SLAB)] * 2,
        out_specs=[blk(CHUNK_LANES, STATE_LANES), blk(CHUNK_LANES + STATE_LANES, CHUNK_LANES),
                   pl.BlockSpec((None, None, 8, SUBLANES, SLAB_STATES), lambda d, s: (d, s, 0, 0, 0))],
        out_shape=[jax.ShapeDtypeStruct((depth, N_SLABS, CHUNK_LANES, STATE_LANES), BF16),
                   jax.ShapeDtypeStruct((depth, N_SLABS, CHUNK_LANES + STATE_LANES, CHUNK_LANES), BF16),
                   jax.ShapeDtypeStruct((depth, N_SLABS, 8, SUBLANES, SLAB_STATES), F32)],
        compiler_params=_params(("arbitrary", "arbitrary")),
        name="s5_prep",
    )(row(lam_re), row(lam_im), dt, b_diag(b_re), b_diag(b_im), c_diag(c_re), c_diag(c_im))


def _s5_kernel(u_ref, w_ref, mo_ref, coef_ref, d_ref, o_ref, ubuf_ref):
    for bb in range(u_ref.shape[0]):
        _s5_sequence(bb, u_ref, w_ref, mo_ref, coef_ref, d_ref, o_ref, ubuf_ref)


def _s5_sequence(bb, u_ref, w_ref, mo_ref, coef_ref, d_ref, o_ref, ubuf_ref):
    nl = S5_CHUNK
    seq = u_ref.shape[1]
    n = seq // nl
    nv = n // SUBLANES
    ns = SLAB_STATES
    ubuf_ref[bb, 0:nl, :] = jnp.zeros((nl, SLAB), F32)
    ubuf_ref[bb, nl:, :] = u_ref[bb]
    cur = [ubuf_ref[bb, pl.ds(nl + j, n, stride=nl), :] for j in range(nl)]
    prev = [ubuf_ref[bb, pl.ds(j, n, stride=nl), :] for j in range(nl)]
    u2 = jnp.concatenate([t.astype(BF16) for t in cur], axis=1)
    u2p = jnp.concatenate([t.astype(BF16) for t in prev], axis=1)
    sc = jnp.dot(u2p, w_ref[...], preferred_element_type=F32)
    xr = sc[:, :ns].reshape(nv, SUBLANES, ns)
    xi = sc[:, ns:].reshape(nv, SUBLANES, ns)
    for idx, k in enumerate((1, 2, 4)):
        cr = coef_ref[2 * idx]
        ci = coef_ref[2 * idx + 1]
        sr = pltpu.roll(xr, k, 1)
        si = pltpu.roll(xi, k, 1)
        xr, xi = xr + (cr * sr - ci * si), xi + (cr * si + ci * sr)
    pr = coef_ref[6]
    pi = coef_ref[7]
    c_r = jnp.zeros((1, ns), F32)
    c_i = jnp.zeros((1, ns), F32)
    out_r, out_i = [], []
    for r in range(nv):
        yr = xr[r] + (pr * c_r - pi * c_i)
        yi = xi[r] + (pr * c_i + pi * c_r)
        out_r.append(yr)
        out_i.append(yi)
        c_r = yr[SUBLANES - 1:SUBLANES, :]
        c_i = yi[SUBLANES - 1:SUBLANES, :]
    x = jnp.concatenate([jnp.concatenate(out_r, axis=0), jnp.concatenate(out_i, axis=0)], axis=1)
    lhs = jnp.concatenate([u2, x.astype(BF16)], axis=1)
    y = jnp.dot(lhs, mo_ref[...], preferred_element_type=F32)
    d = d_ref[...]
    for j in range(nl):
        o_ref[bb, pl.ds(j, n, stride=nl), :] = jax.nn.gelu(y[:, j * SLAB:(j + 1) * SLAB] + d * cur[j]).astype(o_ref.dtype)


def _s5(zu, uc_col, wmat, mo, coef, d_skip, layer, bsz, seq, nb):
    assert seq % (S5_CHUNK * SUBLANES) == 0 and bsz % nb == 0
    zu3 = zu.reshape(bsz, seq, zu.shape[-1])
    c0 = uc_col // SLAB
    out = pl.pallas_call(
        _s5_kernel,
        grid=(N_SLABS, bsz // nb),
        in_specs=[
            pl.BlockSpec((nb, seq, SLAB), lambda s, b: (b, 0, c0 + s)),
            pl.BlockSpec((None, None, CHUNK_LANES, STATE_LANES), lambda s, b: (layer, s, 0, 0)),
            pl.BlockSpec((None, None, CHUNK_LANES + STATE_LANES, CHUNK_LANES), lambda s, b: (layer, s, 0, 0)),
            pl.BlockSpec((None, None, 8, SUBLANES, SLAB_STATES), lambda s, b: (layer, s, 0, 0, 0)),
            pl.BlockSpec((None, 1, SLAB), lambda s, b: (layer, 0, s)),
        ],
        out_specs=pl.BlockSpec((nb, seq, SLAB), lambda s, b: (b, 0, s)),
        out_shape=jax.ShapeDtypeStruct((bsz, seq, WIDTH_C), F32),
        scratch_shapes=[pltpu.VMEM((nb, seq + S5_CHUNK, SLAB), F32)],
        compiler_params=_params(("arbitrary", "arbitrary")),
        name="s5_scan",
    )(zu3, wmat, mo, coef, d_skip)
    return out.reshape(bsz * seq, WIDTH_C)


def _merge_kernel(o0_ref, o1_ref, o2_ref, l0_ref, l1_ref, l2_ref, yb_ref, yc_ref,
                  g0_ref, g1_ref, g2_ref, e_ref, wpa_ref, wpb_ref, wpc_ref, wglu_ref, bglu_ref, out_ref):
    l0, l1, l2 = l0_ref[...], l1_ref[...], l2_ref[...]
    m = jnp.maximum(jnp.maximum(l0, l1), l2)
    e0, e1, e2 = jnp.exp(l0 - m), jnp.exp(l1 - m), jnp.exp(l2 - m)
    den = e0 + e1 + e2
    e = e_ref[...]

    def expand(w):
        hi = w.astype(BF16)
        mid = (w - hi.astype(F32)).astype(BF16)
        return jnp.dot(jnp.concatenate([hi, mid], axis=1), e, preferred_element_type=F32)

    planes = lambda o_ref: jnp.concatenate([o_ref[p] for p in range(o_ref.shape[0])], axis=1)
    ya = (expand(e0 / den) * planes(o0_ref) + expand(e1 / den) * planes(o1_ref) + expand(e2 / den) * planes(o2_ref))
    pa = jnp.dot(ya.astype(BF16), wpa_ref[...], preferred_element_type=F32)
    pb = jnp.dot(yb_ref[...], wpb_ref[...], preferred_element_type=F32)
    yc = yc_ref[...]
    glu = yc * jax.nn.sigmoid(jnp.dot(yc.astype(BF16), wglu_ref[...], preferred_element_type=F32) + bglu_ref[...])
    pc = jnp.dot(glu.astype(BF16), wpc_ref[...], preferred_element_type=F32)
    merged = (g0_ref[...].astype(F32) * pa + g1_ref[...].astype(F32) * pb + g2_ref[...].astype(F32) * pc)
    out_ref[...] = merged.astype(out_ref.dtype)


def _resident(shape, index_map):
    return pl.BlockSpec(shape, index_map, pipeline_mode=pl.Buffered(1))


def _merge(outs, lses, yb, yc, gates, expand, w_pa, w_pb, w_pc, w_glu, b_glu, layer, d_model, tm):
    t = yb.shape[0]
    row = lambda w: pl.BlockSpec((tm, w), lambda i: (i, 0))
    full = lambda a, b: _resident((None, a, b), lambda i: (layer, 0, 0))
    return pl.pallas_call(
        _merge_kernel,
        grid=(t // tm,),
        in_specs=[pl.BlockSpec((WIDTH_A // LANES, tm, LANES), lambda i: (0, i, 0))] * 3
        + [row(LANES)] * 3 + [row(WIDTH_B), row(WIDTH_C)]
        + [pl.BlockSpec((tm, d_model), lambda i, k=k: (i, k)) for k in range(N_BRANCH)]
        + [_resident((2 * LANES, WIDTH_A), lambda i: (0, 0)),
           full(WIDTH_A, d_model), full(WIDTH_B, d_model), full(WIDTH_C, d_model),
           full(WIDTH_C, WIDTH_C), full(1, WIDTH_C)],
        out_specs=pl.BlockSpec((tm, d_model), lambda i: (i, 0)),
        out_shape=jax.ShapeDtypeStruct((t, d_model), BF16),
        compiler_params=_params(("arbitrary",)),
        name="merge",
    )(*outs, *lses, yb, yc, gates, gates, gates, expand, w_pa, w_pb, w_pc, w_glu, b_glu)


def _proj_ln_kernel(m_ref, x_ref, w_ref, g_ref, b_ref, o_ref, ob_ref, *, alpha):
    half = m_ref.shape[0] // 2
    for rs in (slice(0, half), slice(half, 2 * half)):
        f = jnp.dot(m_ref[rs, :], w_ref[...], preferred_element_type=F32)
        y = _layer_norm_rows(alpha * x_ref[rs, :] + f, g_ref[...], b_ref[...])
        o_ref[rs, :] = y
        ob_ref[rs, :] = y.astype(BF16)


def _proj_ln(merged, x, w_o, ln_g, ln_b, layer, alpha, tm):
    t, d = x.shape
    return pl.pallas_call(
        functools.partial(_proj_ln_kernel, alpha=alpha),
        grid=(t // tm,),
        in_specs=[
            pl.BlockSpec((tm, d), lambda i: (i, 0)),
            pl.BlockSpec((tm, d), lambda i: (i, 0)),
            _resident((None, d, d), lambda i: (layer, 0, 0)),
            _resident((None, 1, d), lambda i: (layer, 0, 0)),
            _resident((None, 1, d), lambda i: (layer, 0, 0)),
        ],
        out_specs=[pl.BlockSpec((tm, d), lambda i: (i, 0)), pl.BlockSpec((tm, d), lambda i: (i, 0))],
        out_shape=[jax.ShapeDtypeStruct((t, d), F32), jax.ShapeDtypeStruct((t, d), BF16)],
        compiler_params=_params(("arbitrary",)),
        name="out_proj_ln",
    )(merged, x, w_o, ln_g, ln_b)


def _swiglu_kernel(x_ref, wg_ref, wu_ref, o_ref, wgb_ref, wub_ref):
    @pl.when(pl.program_id(1) == 0)
    def _():
        wgb_ref[...] = wg_ref[...].astype(BF16)
        wub_ref[...] = wu_ref[...].astype(BF16)

    x = x_ref[...]
    gate = jnp.dot(x, wgb_ref[...], preferred_element_type=F32)
    up = jnp.dot(x, wub_ref[...], preferred_element_type=F32)
    o_ref[...] = (jax.nn.silu(gate) * up).astype(o_ref.dtype)


def _swiglu(xb, w_in, layer, d_ff, bm, bn):
    t, d = xb.shape
    nj = d_ff // bn
    return pl.pallas_call(
        _swiglu_kernel,
        grid=(nj, t // bm),
        in_specs=[
            pl.BlockSpec((bm, d), lambda j, i: (i, 0)),
            pl.BlockSpec((None, d, bn), lambda j, i: (layer, 0, j)),
            pl.BlockSpec((None, d, bn), lambda j, i: (layer, 0, j + nj)),
        ],
        out_specs=pl.BlockSpec((bm, bn), lambda j, i: (i, j)),
        out_shape=jax.ShapeDtypeStruct((t, d_ff), BF16),
        scratch_shapes=[pltpu.VMEM((d, bn), BF16), pltpu.VMEM((d, bn), BF16)],
        compiler_params=_params(("arbitrary", "arbitrary"), VMEM_LIMIT_LARGE),
        name="ffn_in",
    )(xb, w_in, w_in)


def _ffn_out_ln_kernel(h_ref, w_ref, x_ref, g_ref, b_ref, o_ref, ob_ref, *, alpha, nk):
    k = pl.program_id(1)

    @pl.when(k == 0)
    def _():
        o_ref[...] = jnp.dot(h_ref[...], w_ref[...], preferred_element_type=F32)

    @pl.when(jnp.logical_and(k > 0, k < nk - 1))
    def _():
        o_ref[...] += jnp.dot(h_ref[...], w_ref[...], preferred_element_type=F32)

    @pl.when(k == nk - 1)
    def _():
        half = h_ref.shape[0] // 2
        for rs in (slice(0, half), slice(half, 2 * half)):
            part = jnp.dot(h_ref[rs, :], w_ref[...], preferred_element_type=F32)
            y = _layer_norm_rows(alpha * x_ref[rs, :] + (o_ref[rs, :] + part), g_ref[...], b_ref[...])
            o_ref[rs, :] = y
            ob_ref[rs, :] = y.astype(BF16)


def _ffn_out_ln(h, x, w_out, ln_g, ln_b, layer, alpha, bm, bk):
    t, d = x.shape
    d_ff = h.shape[1]
    nk = d_ff // bk
    return pl.pallas_call(
        functools.partial(_ffn_out_ln_kernel, alpha=alpha, nk=nk),
        grid=(t // bm, nk),
        in_specs=[
            pl.BlockSpec((bm, bk), lambda i, k: (i, k)),
            pl.BlockSpec((None, bk, d), lambda i, k: (layer, k, 0)),
            pl.BlockSpec((bm, d), lambda i, k: (i, 0)),
            pl.BlockSpec((None, 1, d), lambda i, k: (layer, 0, 0)),
            pl.BlockSpec((None, 1, d), lambda i, k: (layer, 0, 0)),
        ],
        out_specs=[pl.BlockSpec((bm, d), lambda i, k: (i, 0)), pl.BlockSpec((bm, d), lambda i, k: (i, 0))],
        out_shape=[jax.ShapeDtypeStruct((t, d), F32), jax.ShapeDtypeStruct((t, d), BF16)],
        compiler_params=_params(("arbitrary", "arbitrary"), VMEM_LIMIT_LARGE),
        name="ffn_out_ln",
    )(h, w_out, x, ln_g, ln_b)


@jax.jit
def kernel(x, w_in, b_in, rel_bias, sgu_ln_g, sgu_ln_b, w_s, b_s, lam_re, lam_im, log_dt, b_re, b_im, c_re, c_im, d_skip, w_glu, b_glu, w_pa, w_pb, w_pc, w_o, ln1_g, ln1_b, w_ffn_in, w_ffn_out, ln2_g, ln2_b):
    bsz, seq, d_model = x.shape
    depth = w_in.shape[0]
    d_ff = w_ffn_out.shape[1]
    t = bsz * seq
    alpha = (2 * depth) ** 0.25
    in_cols = w_in.shape[-1]
    gl_width = N_BRANCH * d_model
    zb_src = 3 * QKV_WIDTH
    uc_src = zb_src + 2 * WIDTH_B
    gl_src = uc_src + WIDTH_C
    assert gl_src + gl_width == in_cols
    bn = 768

    bf = lambda a: a.astype(BF16)
    row3 = lambda a: a.reshape(a.shape[0], 1, a.shape[1]).astype(F32)
    w_pa_b, w_pb_b, w_pc_b, w_glu_b = bf(w_pa), bf(w_pb), bf(w_pc), bf(w_glu)
    w_o_b, w_ffn_out_b = bf(w_o), bf(w_ffn_out)
    w_in32, w_ffn_in32 = w_in.astype(F32), w_ffn_in.astype(F32)
    b_in3, b_glu3, d_skip3 = row3(b_in), row3(b_glu), row3(d_skip)
    sgu_g3, sgu_b3 = row3(sgu_ln_g), row3(sgu_ln_b)
    ln1_g3, ln1_b3, ln2_g3, ln2_b3 = row3(ln1_g), row3(ln1_b), row3(ln2_g), row3(ln2_b)
    w_s32 = w_s.astype(F32)

    group_bias = [_group_rel_bias(rel_bias, g, dil, win // dil) for g, (win, dil) in enumerate(ATT_PATTERNS)]
    expand = jnp.asarray(np.tile(np.kron(np.eye(LANES, HEADS_PER_GROUP, dtype=np.float32),
                                         np.ones((1, HEAD_DIM), np.float32)), (2, 1)), dtype=BF16)

    s5_w, s5_mo, s5_coef = _s5_prep(lam_re, lam_im, log_dt, b_re, b_im, c_re, c_im)

    xf = x.reshape(t, d_model).astype(F32)
    xb = bf(xf)
    for l in range(depth):
        qkv = _in_proj(xb, w_in32, b_in3, l, lambda j: j, 3 * QKV_WIDTH, BF16, TOKEN_TILE, WIDTH_A, "streams")
        uc = _in_proj(xb, w_in32, b_in3, l, lambda j: j + uc_src // bn, WIDTH_C, F32, TOKEN_TILE // 2, bn, "plain")
        gates = _in_proj(xb, w_in32, b_in3, l, lambda j: j + gl_src // bn, gl_width, BF16, TOKEN_TILE, bn,
                         "sigmoid")
        outs, lses = [], []
        for g, (window, dil) in enumerate(ATT_PATTERNS):
            o_g, lse_g = _attention_group(qkv, group_bias[g], g, dil, bsz, seq)
            outs.append(o_g)
            lses.append(lse_g)
        yb = _spatial_gating(xb, w_in32, b_in3, zb_src, sgu_g3, sgu_b3, w_s32, b_s, l, 1024, 2)
        yc = _s5(uc, 0, s5_w, s5_mo, s5_coef, d_skip3, l, bsz, seq, 1)
        merged = _merge(outs, lses, yb, yc, gates, expand, w_pa_b, w_pb_b, w_pc_b, w_glu_b, b_glu3,
                        l, d_model, 512)
        xf, xb = _proj_ln(merged, xf, w_o_b, ln1_g3, ln1_b3, l, alpha, 512)
        h = _swiglu(xb, w_ffn_in32, l, d_ff, 1024, 512)
        xf, xb = _ffn_out_ln(h, xf, w_ffn_out_b, ln2_g3, ln2_b3, l, alpha, 512, d_ff // 2)
    return xf.reshape(bsz, seq, d_model).astype(x.dtype)
```
